```python
import math
import jax, jax.numpy as jnp
from jax import lax
import numpy as np

D_MODEL = 1024
BATCH = 1
SEQ = 16384
DEPTH = 4

GRID_W = 64
N_META = 16
EPS = 1e-6
ATTN_HEADS = 8
ATTN_KV_HEADS = 2
HEAD_DIM = 128
KV_REP = ATTN_HEADS // ATTN_KV_HEADS
ROPE_AXIS_DIM = HEAD_DIM // 2
ROPE_THETA = 10000.0
Q_BLOCK = 128
ATTN_WIDTH = ATTN_HEADS * HEAD_DIM
KV_WIDTH = ATTN_KV_HEADS * HEAD_DIM
S5_WIDTH = D_MODEL
S5_GROUP = 16
S5_GROUPS = S5_WIDTH // S5_GROUP
S5_STATE = 64
EVEN_IN_WIDTH = 2 * ATTN_WIDTH + 2 * KV_WIDTH + 2 * S5_WIDTH
EVEN_SPLITS = (ATTN_WIDTH, ATTN_WIDTH + KV_WIDTH, ATTN_WIDTH + 2 * KV_WIDTH,
               2 * ATTN_WIDTH + 2 * KV_WIDTH, 2 * ATTN_WIDTH + 2 * KV_WIDTH + S5_WIDTH)
MLSTM_WIDTH = 2 * D_MODEL
MLSTM_HEADS = 8
MLSTM_HEAD_V = MLSTM_WIDTH // MLSTM_HEADS
MLSTM_HEAD_QK = MLSTM_HEAD_V // 2
CONV_W = 5
CHUNK = 64
PAD_FRONT = CHUNK - N_META
NEG_BIG = -1e9
N_EVEN = (DEPTH + 1) // 2
N_ODD = DEPTH // 2

kernel_name = 'hybrid_attn_s5_mlstm_encoder'


def rms_norm(x, g):
    xf = x.astype(jnp.float32)
    y = xf * lax.rsqrt(jnp.mean(xf * xf, axis=-1, keepdims=True) + EPS)
    return (y * g.astype(jnp.float32)).astype(x.dtype)


def axial_rope_tables(n_real):
    rows = n_real // GRID_W
    row = jnp.repeat(jnp.arange(rows, dtype=jnp.float32), GRID_W)
    col = jnp.tile(jnp.arange(GRID_W, dtype=jnp.float32), rows)
    meta = jnp.zeros((N_META,), jnp.float32)
    row = jnp.concatenate([meta, row])
    col = jnp.concatenate([meta, col])
    freqs = ROPE_THETA ** (-jnp.arange(0, ROPE_AXIS_DIM, 2, dtype=jnp.float32) / ROPE_AXIS_DIM)
    ang_r = row[:, None] * freqs[None]
    ang_c = col[:, None] * freqs[None]
    shp = lambda a: a[None, :, None, :]
    return shp(jnp.cos(ang_r)), shp(jnp.sin(ang_r)), shp(jnp.cos(ang_c)), shp(jnp.sin(ang_c))


def apply_axial_rope(x, tabs):
    cos_r, sin_r, cos_c, sin_c = tabs
    xf = x.astype(jnp.float32)

    def rot(xs, cos, sin):
        half = xs.shape[-1] // 2
        x1, x2 = xs[..., :half], xs[..., half:]
        return jnp.concatenate([x1 * cos - x2 * sin, x2 * cos + x1 * sin], axis=-1)

    out = jnp.concatenate([rot(xf[..., :ROPE_AXIS_DIM], cos_r, sin_r),
                           rot(xf[..., ROPE_AXIS_DIM:], cos_c, sin_c)], axis=-1)
    return out.astype(x.dtype)


def block_attention(q, k, v):
    b, l = q.shape[0], q.shape[1]
    n_blocks = -(-l // Q_BLOCK)
    pad = n_blocks * Q_BLOCK - l
    qp = jnp.pad(q, ((0, 0), (0, pad), (0, 0), (0, 0)))
    qb = qp.reshape(b, n_blocks, Q_BLOCK, ATTN_KV_HEADS, KV_REP, HEAD_DIM).transpose(1, 0, 3, 4, 2, 5)
    kt = k.transpose(0, 2, 1, 3)
    vt = v.transpose(0, 2, 1, 3)
    scale = HEAD_DIM ** -0.5

    def one_block(qblk):
        s = jnp.einsum('bgrqd,bgkd->bgrqk', qblk, kt).astype(jnp.float32) * scale
        p = jax.nn.softmax(s, axis=-1).astype(vt.dtype)
        return jnp.einsum('bgrqk,bgkd->bgrqd', p, vt)

    o = lax.map(one_block, qb)
    o = o.transpose(1, 0, 4, 2, 3, 5).reshape(b, n_blocks * Q_BLOCK, ATTN_WIDTH)
    return o[:, :l]


def _linear_combine(e1, e2):
    a1, b1 = e1
    a2, b2 = e2
    return a1 * a2, a2 * b1 + b2


def s5_bidirectional(u, a_re, a_im, log_dt, b_re, b_im, c_re, c_im, d_skip, w_glu, b_glu):
    bsz, l, _ = u.shape
    f32 = jnp.float32
    uf = u.astype(f32)
    ug = uf.reshape(bsz, l, S5_GROUPS, S5_GROUP)
    y = uf * d_skip.astype(f32)
    for direction, reverse in ((0, False), (1, True)):
        a = lax.complex(a_re[direction].astype(f32), a_im[direction].astype(f32))
        dt = jnp.exp(log_dt[direction].astype(f32))[:, None]
        a_bar = jnp.exp(a * dt)
        b_bar = ((a_bar - 1.0) / a)[..., None] * lax.complex(b_re[direction].astype(f32),
                                                           b_im[direction].astype(f32))
        bu = jnp.einsum('blgp,gnp->blgn', ug.astype(jnp.complex64), b_bar)
        a_seq = jnp.broadcast_to(a_bar, bu.shape)
        _, states = lax.associative_scan(_linear_combine, (a_seq, bu), axis=1, reverse=reverse)
        c = lax.complex(c_re[direction].astype(f32), c_im[direction].astype(f32))
        y = y + jnp.einsum('blgn,gpn->blgp', states, c).real.reshape(bsz, l, S5_WIDTH)
    y = jax.nn.gelu(y)
    y = y * jax.nn.sigmoid(y @ w_glu.astype(f32) + b_glu.astype(f32))
    return y.astype(u.dtype)


def mlstm_chunkwise(q, k, v, log_i, log_f):
    bsz, nh, t, dk = q.shape
    dv = v.shape[-1]
    nc = t // CHUNK
    q = q.reshape(bsz, nh, nc, CHUNK, dk) * (dk ** -0.5)
    k = k.reshape(bsz, nh, nc, CHUNK, dk)
    v = v.reshape(bsz, nh, nc, CHUNK, dv)
    li = log_i.reshape(bsz, nh, nc, CHUNK)
    lf = log_f.reshape(bsz, nh, nc, CHUNK)
    b = jnp.cumsum(lf, axis=-1)
    b_last = b[..., -1]
    g = b_last[..., None] - b + li
    g_max = jnp.max(g, axis=-1)

    def step(carry, xs):
        c_st, n_st, m_st = carry
        k_c, v_c, g_c, gmax_c, bl_c = xs
        m_new = jnp.maximum(bl_c + m_st, gmax_c)
        decay = jnp.exp(bl_c + m_st - m_new)
        w = jnp.exp(g_c - m_new[..., None])
        c_new = decay[..., None, None] * c_st + jnp.einsum('bhs,bhsv,bhsk->bhvk', w, v_c, k_c)
        n_new = decay[..., None] * n_st + jnp.einsum('bhs,bhsk->bhk', w, k_c)
        return (c_new, n_new, m_new), (c_st, n_st, m_st)

    init = (jnp.zeros((bsz, nh, dv, dk), jnp.float32), jnp.zeros((bsz, nh, dk), jnp.float32),
            jnp.zeros((bsz, nh), jnp.float32))
    xs = (jnp.moveaxis(k, 2, 0), jnp.moveaxis(v, 2, 0), jnp.moveaxis(g, 2, 0),
          jnp.moveaxis(g_max, 2, 0), jnp.moveaxis(b_last, 2, 0))
    _, (c_prev, n_prev, m_prev) = lax.scan(step, init, xs)
    m_prev = jnp.moveaxis(m_prev, 0, 2)

    d = b[..., :, None] - b[..., None, :] + li[..., None, :]
    causal_in_chunk = jnp.tril(jnp.ones((CHUNK, CHUNK), dtype=bool))
    d = jnp.where(causal_in_chunk, d, -jnp.inf)
    inter_log = b + m_prev[..., None]
    m_t = jnp.maximum(jnp.max(d, axis=-1), inter_log)
    sw = jnp.exp(d - m_t[..., None]) * jnp.einsum('bhntd,bhnsd->bhnts', q, k)
    inter_scale = jnp.exp(inter_log - m_t)
    num = jnp.einsum('bhnts,bhnsv->bhntv', sw, v) + \
        inter_scale[..., None] * jnp.einsum('nbhvk,bhntk->bhntv', c_prev, q)
    den = jnp.sum(sw, axis=-1) + inter_scale * jnp.einsum('nbhk,bhntk->bhnt', n_prev, q)
    h = num / jnp.maximum(jnp.abs(den), jnp.exp(-m_t))[..., None]
    return h.reshape(bsz, nh, t, dv)


def attn_s5_layer(h, tabs, w_in, q_norm, k_norm, a_re, a_im, log_dt, b_re, b_im, c_re, c_im,
                  d_skip, w_glu, b_glu, w_out):
    bsz, l, _ = h.shape
    proj = h @ w_in
    q, k, v, gate_a, u, gate_b = jnp.split(proj, EVEN_SPLITS, axis=-1)
    q = apply_axial_rope(rms_norm(q.reshape(bsz, l, ATTN_HEADS, HEAD_DIM), q_norm), tabs)
    k = apply_axial_rope(rms_norm(k.reshape(bsz, l, ATTN_KV_HEADS, HEAD_DIM), k_norm), tabs)
    v = v.reshape(bsz, l, ATTN_KV_HEADS, HEAD_DIM)
    y_a = block_attention(q, k, v) * jax.nn.silu(gate_a)
    y_b = s5_bidirectional(u, a_re, a_im, log_dt, b_re, b_im, c_re, c_im, d_skip, w_glu, b_glu) \
        * jax.nn.silu(gate_b)
    return jnp.concatenate([y_a, y_b], axis=-1) @ w_out


def mlstm_layer(h, w_in, conv_w, conv_b, w_q, w_k, w_v, w_igate, b_igate, w_fgate, b_fgate,
                w_ogate, b_ogate, head_norm, skip, w_out):
    bsz, l, _ = h.shape
    f32 = jnp.float32
    proj = h @ w_in
    xm, z = jnp.split(proj, [MLSTM_WIDTH], axis=-1)
    conv = lax.conv_general_dilated(xm, conv_w[:, None, :].astype(xm.dtype), window_strides=(1,),
                                    padding=[(CONV_W // 2, CONV_W // 2)],
                                    dimension_numbers=('NWC', 'WIO', 'NWC'),
                                    feature_group_count=MLSTM_WIDTH)
    xc = jax.nn.silu(conv + conv_b)
    xch = xc.reshape(bsz, l, MLSTM_HEADS, MLSTM_HEAD_V)
    q = jnp.einsum('blhc,hcd->bhld', xch, w_q).astype(f32)
    k = jnp.einsum('blhc,hcd->bhld', xch, w_k).astype(f32)
    v = jnp.einsum('blhc,hcd->bhld', xm.reshape(bsz, l, MLSTM_HEADS, MLSTM_HEAD_V), w_v).astype(f32)
    li = (jnp.einsum('blc,dch->dbhl', xc, w_igate) + b_igate[:, None, :, None]).astype(f32)
    lf = jax.nn.log_sigmoid((jnp.einsum('blc,dch->dbhl', xc, w_fgate)
                             + b_fgate[:, None, :, None]).astype(f32))
    pad4 = ((0, 0), (0, 0), (PAD_FRONT, 0), (0, 0))
    pad3 = ((0, 0), (0, 0), (PAD_FRONT, 0))
    q, k, v = jnp.pad(q, pad4), jnp.pad(k, pad4), jnp.pad(v, pad4)
    li_f, li_b = jnp.pad(li[0], pad3, constant_values=NEG_BIG), jnp.pad(li[1], pad3, constant_values=NEG_BIG)
    lf_f, lf_b = jnp.pad(lf[0], pad3), jnp.pad(lf[1], pad3)
    flip = lambda a: jnp.flip(a, axis=2)
    h_f = mlstm_chunkwise(q, k, v, li_f, lf_f)
    h_b = flip(mlstm_chunkwise(flip(q), flip(k), flip(v), flip(li_b), flip(lf_b)))
    cell = (h_f + h_b)[:, :, PAD_FRONT:].transpose(0, 2, 1, 3)
    cell = rms_norm(cell, head_norm.reshape(MLSTM_HEADS, MLSTM_HEAD_V)).reshape(bsz, l, MLSTM_WIDTH)
    o = jax.nn.sigmoid(xc @ w_ogate + b_ogate)
    hh = o * cell.astype(xc.dtype) + skip * xc
    return (hh * jax.nn.silu(z)) @ w_out


def setup_inputs(seed: int = 0) -> dict:
    key = jax.random.key(seed)
    ks = iter(jax.random.split(key, 48))
    nrm = lambda shape, scale: jax.random.normal(next(ks), shape, jnp.float32) * scale
    gain = lambda shape: 1.0 + nrm(shape, 0.05)
    a_im0 = jnp.pi * jnp.arange(S5_STATE, dtype=jnp.float32)
    return {
        'x': nrm((BATCH, SEQ, D_MODEL), 1.0),
        'meta_tokens': nrm((N_META, D_MODEL), 1.0),
        'norm_pre': gain((DEPTH, D_MODEL)),
        'norm_post': gain((DEPTH, D_MODEL)),
        'w_in_even': nrm((N_EVEN, D_MODEL, EVEN_IN_WIDTH), D_MODEL ** -0.5),
        'q_norm': gain((N_EVEN, HEAD_DIM)),
        'k_norm': gain((N_EVEN, HEAD_DIM)),
        'ssm_a_re': -0.5 + nrm((N_EVEN, 2, S5_GROUPS, S5_STATE), 0.01),
        'ssm_a_im': a_im0 + nrm((N_EVEN, 2, S5_GROUPS, S5_STATE), 0.01),
        'ssm_log_dt': jax.random.uniform(next(ks), (N_EVEN, 2, S5_GROUPS), jnp.float32,
                                         math.log(1e-3), math.log(1e-1)),
        'ssm_b_re': nrm((N_EVEN, 2, S5_GROUPS, S5_STATE, S5_GROUP), (2 * S5_GROUP) ** -0.5),
        'ssm_b_im': nrm((N_EVEN, 2, S5_GROUPS, S5_STATE, S5_GROUP), (2 * S5_GROUP) ** -0.5),
        'ssm_c_re': nrm((N_EVEN, 2, S5_GROUPS, S5_GROUP, S5_STATE), S5_STATE ** -0.5),
        'ssm_c_im': nrm((N_EVEN, 2, S5_GROUPS, S5_GROUP, S5_STATE), S5_STATE ** -0.5),
        'ssm_d': nrm((N_EVEN, S5_WIDTH), 1.0),
        'w_glu': nrm((N_EVEN, S5_WIDTH, S5_WIDTH), S5_WIDTH ** -0.5),
        'b_glu': nrm((N_EVEN, S5_WIDTH), 0.02),
        'w_out_even': nrm((N_EVEN, ATTN_WIDTH + S5_WIDTH, D_MODEL), (ATTN_WIDTH + S5_WIDTH) ** -0.5),
        'w_in_odd': nrm((N_ODD, D_MODEL, 2 * MLSTM_WIDTH), D_MODEL ** -0.5),
        'conv_w': nrm((N_ODD, CONV_W, MLSTM_WIDTH), CONV_W ** -0.5),
        'conv_b': nrm((N_ODD, MLSTM_WIDTH), 0.02),
        'w_q': nrm((N_ODD, MLSTM_HEADS, MLSTM_HEAD_V, MLSTM_HEAD_QK), MLSTM_HEAD_V ** -0.5),
        'w_k': nrm((N_ODD, MLSTM_HEADS, MLSTM_HEAD_V, MLSTM_HEAD_QK), MLSTM_HEAD_V ** -0.5),
        'w_v': nrm((N_ODD, MLSTM_HEADS, MLSTM_HEAD_V, MLSTM_HEAD_V), MLSTM_HEAD_V ** -0.5),
        'w_igate': nrm((N_ODD, 2, MLSTM_WIDTH, MLSTM_HEADS), MLSTM_WIDTH ** -0.5),
        'b_igate': nrm((N_ODD, 2, MLSTM_HEADS), 0.1),
        'w_fgate': nrm((N_ODD, 2, MLSTM_WIDTH, MLSTM_HEADS), MLSTM_WIDTH ** -0.5),
        'b_fgate': jnp.linspace(3.0, 6.0, MLSTM_HEADS, dtype=jnp.float32) + nrm((N_ODD, 2, MLSTM_HEADS), 0.1),
        'w_ogate': nrm((N_ODD, MLSTM_WIDTH, MLSTM_WIDTH), MLSTM_WIDTH ** -0.5),
        'b_ogate': nrm((N_ODD, MLSTM_WIDTH), 0.02),
        'head_norm': gain((N_ODD, MLSTM_WIDTH)),
        'mlstm_skip': gain((N_ODD, MLSTM_WIDTH)),
        'w_out_odd': nrm((N_ODD, MLSTM_WIDTH, D_MODEL), MLSTM_WIDTH ** -0.5),
    }


def reference(x, meta_tokens, norm_pre, norm_post, w_in_even, q_norm, k_norm, ssm_a_re, ssm_a_im,
              ssm_log_dt, ssm_b_re, ssm_b_im, ssm_c_re, ssm_c_im, ssm_d, w_glu, b_glu, w_out_even,
              w_in_odd, conv_w, conv_b, w_q, w_k, w_v, w_igate, b_igate, w_fgate, b_fgate,
              w_ogate, b_ogate, head_norm, mlstm_skip, w_out_odd):
    bsz, n_real, _ = x.shape
    meta = jnp.broadcast_to(meta_tokens[None].astype(x.dtype), (bsz, N_META, D_MODEL))
    h_res = jnp.concatenate([meta, x], axis=1)
    tabs = axial_rope_tables(n_real)
    for layer in range(DEPTH):
        i = layer // 2
        h = rms_norm(h_res, norm_pre[layer])
        if layer % 2 == 0:
            y = attn_s5_layer(h, tabs, w_in_even[i], q_norm[i], k_norm[i], ssm_a_re[i], ssm_a_im[i],
                              ssm_log_dt[i], ssm_b_re[i], ssm_b_im[i], ssm_c_re[i], ssm_c_im[i],
                              ssm_d[i], w_glu[i], b_glu[i], w_out_even[i])
        else:
            y = mlstm_layer(h, w_in_odd[i], conv_w[i], conv_b[i], w_q[i], w_k[i], w_v[i],
                            w_igate[i], b_igate[i], w_fgate[i], b_fgate[i], w_ogate[i], b_ogate[i],
                            head_norm[i], mlstm_skip[i], w_out_odd[i])
        h_res = h_res + rms_norm(y, norm_post[layer])
    return h_res[:, N_META:]
```

```python
import functools
import math

import jax
import jax.numpy as jnp
from jax import lax
from jax.experimental import pallas as pl
from jax.experimental.pallas import tpu as pltpu

F32 = jnp.float32
BF16 = jnp.bfloat16

D_MODEL = 1024
N_META = 16
FRONT = 128
N_DUMMY = FRONT - N_META
GRID_W = 64
EPS = 1e-6
NEG = -1e30

HEADS = 8
KV_HEADS = 2
KV_REP = HEADS // KV_HEADS
HEAD_DIM = 128
ROPE_AXIS = HEAD_DIM // 2
ROPE_THETA = 10000.0
ATTN_W = HEADS * HEAD_DIM
KV_W = KV_HEADS * HEAD_DIM
TQ = 128
TK = 512

S5_W = 1024
S5_P = 16
S5_G = S5_W // S5_P
S5_N = 64
S5_T = 16
S5_GB = 8
S5_PB = S5_GB // 2
S5_CW = S5_T * S5_P

ML_W = 2048
ML_H = 8
ML_DV = 256
ML_DK = 128
CONV_W = 5
ML_CHUNK = 128
NEG_BIG = -1e9

TM = 384
VMEM_LIMIT = 56 * 1024 * 1024


def _cparams(*sem):
    return pltpu.CompilerParams(dimension_semantics=sem, vmem_limit_bytes=VMEM_LIMIT)


def _rms(x, g):
    return x * lax.rsqrt(jnp.mean(x * x, axis=-1, keepdims=True) + EPS) * g


def _sigmoid(x):
    return 1.0 / (1.0 + jnp.exp(-x))


def _silu(x):
    return x * _sigmoid(x)


def _gelu_tanh(x):
    return 0.5 * x * (1.0 + jnp.tanh(math.sqrt(2.0 / math.pi) * (x + 0.044715 * (x * x * x))))


def _dot(a, b):
    return jnp.dot(a, b, preferred_element_type=F32)


def _dot_nt(a, b):
    return lax.dot_general(a, b, (((1,), (1,)), ((), ())), preferred_element_type=F32)


def _dot_tn(a, b):
    return lax.dot_general(a, b, (((0,), (0,)), ((), ())), preferred_element_type=F32)


def _full(shape):
    return pl.BlockSpec(shape, lambda *_: (0,) * len(shape))


def _even_in_kernel(h_ref, gpre_ref, w_ref, qg_ref, kg_ref, cos_ref, sin_ref,
                    q_ref, k_ref, v_ref, ga_ref, u_ref, gb_ref):
    hn = _rms(h_ref[...], gpre_ref[...]).astype(BF16)
    cos = cos_ref[...]
    sin = sin_ref[...]
    lane = lax.broadcasted_iota(jnp.int32, cos.shape, 1)
    first_half = (lane % ROPE_AXIS) < (ROPE_AXIS // 2)

    def head(x, g):
        y = _rms(x, g)
        partner = jnp.where(first_half, pltpu.roll(y, HEAD_DIM - ROPE_AXIS // 2, 1),
                            pltpu.roll(y, ROPE_AXIS // 2, 1))
        return (y * cos + partner * sin).astype(BF16)

    qsec = _dot(hn, w_ref[:, 0:ATTN_W])
    for h in range(HEADS):
        sl = slice(h * HEAD_DIM, (h + 1) * HEAD_DIM)
        q_ref[:, sl] = head(qsec[:, sl], qg_ref[...])
    ksec = _dot(hn, w_ref[:, ATTN_W:ATTN_W + KV_W])
    for h in range(KV_HEADS):
        sl = slice(h * HEAD_DIM, (h + 1) * HEAD_DIM)
        k_ref[:, sl] = head(ksec[:, sl], kg_ref[...])
    c0 = ATTN_W + KV_W
    v_ref[...] = _dot(hn, w_ref[:, c0:c0 + KV_W]).astype(BF16)
    c0 += KV_W
    ga_ref[...] = _dot(hn, w_ref[:, c0:c0 + ATTN_W]).astype(BF16)
    c0 += ATTN_W
    u_ref[...] = _dot(hn, w_ref[:, c0:c0 + S5_W]).astype(BF16)
    c0 += S5_W
    gb_ref[...] = _dot(hn, w_ref[:, c0:c0 + S5_W]).astype(BF16)


def _even_in(hres, gpre, w_in, qg, kg, cos, sin):
    lp = hres.shape[0]
    win_w = w_in.shape[1]
    row = lambda w: pl.BlockSpec((TM, w), lambda i: (i, 0))
    outs = [(ATTN_W, BF16), (KV_W, BF16), (KV_W, BF16), (ATTN_W, BF16), (S5_W, BF16), (S5_W, BF16)]
    return pl.pallas_call(
        _even_in_kernel,
        grid=(lp // TM,),
        in_specs=[row(D_MODEL), _full((1, D_MODEL)), _full((D_MODEL, win_w)),
                  _full((1, HEAD_DIM)), _full((1, HEAD_DIM)), row(HEAD_DIM), row(HEAD_DIM)],
        out_specs=[row(w) for w, _ in outs],
        out_shape=[jax.ShapeDtypeStruct((lp, w), dt) for w, dt in outs],
        compiler_params=_cparams("parallel"),
        name="even_in",
    )(hres, gpre, w_in, qg, kg, cos, sin)


def _attn_kernel(q_ref, k_ref, v_ref, o_ref, m_sc, l_sc, acc_sc, *, n_kv_tiles):
    q4 = jnp.concatenate([q_ref[:, r * HEAD_DIM:(r + 1) * HEAD_DIM] for r in range(KV_REP)], axis=0)

    def update(kb, vb, mask_dummy):
        s = _dot_nt(q4, kb)
        if mask_dummy:
            col = lax.broadcasted_iota(jnp.int32, s.shape, 1)
            s = jnp.where(col >= N_DUMMY, s, NEG)
        m_old = m_sc[...]
        m_new = jnp.maximum(m_old, jnp.max(s, axis=-1, keepdims=True))
        alpha = jnp.exp(m_old - m_new)
        p = jnp.exp(s - m_new)
        l_sc[...] = alpha * l_sc[...] + jnp.sum(p, axis=-1, keepdims=True)
        acc_sc[...] = alpha * acc_sc[...] + _dot(p.astype(BF16), vb)
        m_sc[...] = m_new

    m_sc[...] = jnp.full(m_sc.shape, NEG, F32)
    l_sc[...] = jnp.zeros(l_sc.shape, F32)
    acc_sc[...] = jnp.zeros(acc_sc.shape, F32)
    update(k_ref[0:FRONT, :], v_ref[0:FRONT, :], True)

    def body(j, carry):
        off = pl.multiple_of(FRONT + j * TK, 128)
        update(k_ref[pl.ds(off, TK), :], v_ref[pl.ds(off, TK), :], False)
        return carry

    lax.fori_loop(0, n_kv_tiles, body, 0)
    o = acc_sc[...] / l_sc[...]
    for r in range(KV_REP):
        o_ref[:, r * HEAD_DIM:(r + 1) * HEAD_DIM] = o[r * TQ:(r + 1) * TQ].astype(BF16)


def _attention(q, k, v):
    lp = q.shape[0]
    n_kv_tiles = (lp - FRONT) // TK
    gw = KV_REP * HEAD_DIM
    return pl.pallas_call(
        functools.partial(_attn_kernel, n_kv_tiles=n_kv_tiles),
        grid=(KV_HEADS, lp // TQ),
        in_specs=[pl.BlockSpec((TQ, gw), lambda g, i: (i, g)),
                  pl.BlockSpec((lp, HEAD_DIM), lambda g, i: (0, g)),
                  pl.BlockSpec((lp, HEAD_DIM), lambda g, i: (0, g))],
        out_specs=pl.BlockSpec((TQ, gw), lambda g, i: (i, g)),
        out_shape=jax.ShapeDtypeStruct((lp, ATTN_W), BF16),
        scratch_shapes=[pltpu.VMEM((KV_REP * TQ, 1), F32), pltpu.VMEM((KV_REP * TQ, 1), F32),
                        pltpu.VMEM((KV_REP * TQ, HEAD_DIM), F32)],
        compiler_params=_cparams("parallel", "parallel"),
        name="attention",
    )(q, k, v)


def _s5_tables(a_re, a_im, log_dt, b_re, b_im, c_re, c_im, d_skip):
    t = S5_T
    lag = jnp.arange(t + 1, dtype=F32)
    per_dir = []
    for d in range(2):
        a = lax.complex(a_re[d], a_im[d])
        adt = a * jnp.exp(log_dt[d])[:, None]
        a_bar = jnp.exp(adt)
        b_bar = ((a_bar - 1.0) / a)[..., None] * lax.complex(b_re[d], b_im[d])
        c = lax.complex(c_re[d], c_im[d])
        pw = jnp.exp(adt[None] * lag[:, None, None].astype(jnp.complex64))
        klag = jnp.real(jnp.einsum('gon,dgn,gni->dgoi', c, pw[:t], b_bar))
        per_dir.append((b_bar, c, pw, klag))
    (bb0, c0, pw0, kl0), (bb1, c1, pw1, kl1) = per_dir
    ii = jnp.arange(t)[:, None]
    jj = jnp.arange(t)[None, :]
    fwd = jnp.where((jj >= ii)[..., None, None, None], kl0[jnp.clip(jj - ii, 0, t - 1)], 0.0)
    bwd = jnp.where((ii >= jj)[..., None, None, None], kl1[jnp.clip(ii - jj, 0, t - 1)], 0.0)
    eye_p = jnp.eye(S5_P, dtype=F32)
    skip = (ii == jj)[..., None, None, None] * (d_skip.reshape(S5_G, S5_P)[:, :, None] * eye_p)[None, None]
    kmat = (fwd + bwd + skip).transpose(2, 0, 4, 1, 3).reshape(S5_G, S5_CW, S5_CW)

    idx = jnp.arange(t)
    bf = pw0[t - 1 - idx][..., None] * bb0[None]
    bb = pw1[idx][..., None] * bb1[None]
    cf = c0[None] * pw0[idx + 1][:, :, None, :]
    cb = c1[None] * pw1[t - idx][:, :, None, :]

    def in_mat(x):
        return x.transpose(1, 0, 3, 2).reshape(S5_G, S5_CW, S5_N)

    def out_mat(x):
        return x.transpose(1, 3, 0, 2).reshape(S5_G, S5_N, S5_CW)

    n_pair = S5_G // 2
    zero_in = jnp.zeros((n_pair, S5_CW, S5_N), F32)
    zero_out = jnp.zeros((n_pair, S5_N, S5_CW), F32)

    def pair_in(x):
        e, o = x[0::2], x[1::2]
        return jnp.concatenate([jnp.concatenate([e, zero_in], axis=2),
                                jnp.concatenate([zero_in, o], axis=2)], axis=1)

    def pair_out(x):
        e, o = x[0::2], x[1::2]
        return jnp.concatenate([jnp.concatenate([e, zero_out], axis=2),
                                jnp.concatenate([zero_out, o], axis=2)], axis=1)

    bmat = jnp.concatenate([pair_in(in_mat(jnp.real(bf))), pair_in(in_mat(jnp.imag(bf))),
                            pair_in(in_mat(jnp.real(bb))), pair_in(in_mat(jnp.imag(bb)))], axis=2)
    cmat = jnp.concatenate([pair_out(out_mat(jnp.real(cf))), pair_out(out_mat(-jnp.imag(cf))),
                            pair_out(out_mat(jnp.real(cb))), pair_out(out_mat(-jnp.imag(cb)))], axis=1)

    def pair_row(x):
        return x.reshape(S5_G // S5_GB, 1, S5_PB * 2 * S5_N)

    a_pow = jnp.concatenate([pair_row(jnp.real(pw0[t])), pair_row(jnp.imag(pw0[t])),
                             pair_row(jnp.real(pw1[t])), pair_row(jnp.imag(pw1[t]))], axis=1)
    return kmat.astype(BF16), bmat.astype(BF16), cmat.astype(BF16), a_pow


def _s5_kernel(u_ref, k_ref, b_ref, c_ref, a_ref, y_ref, fre, fim, bre, bim, *, n_chunks):
    pw = 2 * S5_N
    for p in range(S5_PB):
        s = _dot(u_ref[:, p * 2 * S5_CW:(p + 1) * 2 * S5_CW], b_ref[p])
        cols = slice(p * pw, (p + 1) * pw)
        fre[:, cols] = s[:, 0:pw]
        fim[:, cols] = s[:, pw:2 * pw]
        bre[:, cols] = s[:, 2 * pw:3 * pw]
        bim[:, cols] = s[:, 3 * pw:4 * pw]

    far, fai, bar, bai = a_ref[0, 0:1, :], a_ref[0, 1:2, :], a_ref[0, 2:3, :], a_ref[0, 3:4, :]

    def step(i, carry):
        xr, xi, yr, yi = carry
        rf = pl.ds(i, 1)
        rb = pl.ds(n_chunks - 1 - i, 1)
        sr, si = fre[rf, :], fim[rf, :]
        tr, ti = bre[rb, :], bim[rb, :]
        fre[rf, :] = xr
        fim[rf, :] = xi
        bre[rb, :] = yr
        bim[rb, :] = yi
        return (far * xr - fai * xi + sr, far * xi + fai * xr + si,
                bar * yr - bai * yi + tr, bar * yi + bai * yr + ti)

    zero = jnp.zeros((1, S5_PB * pw), F32)
    lax.fori_loop(0, n_chunks, step, (zero, zero, zero, zero))

    for p in range(S5_PB):
        cols = slice(p * pw, (p + 1) * pw)
        state = jnp.concatenate([fre[:, cols], fim[:, cols], bre[:, cols], bim[:, cols]], axis=1).astype(BF16)
        carry_out = _dot(state, c_ref[p])
        for e in range(2):
            g = 2 * p + e
            gc = slice(g * S5_CW, (g + 1) * S5_CW)
            y = _dot(u_ref[:, gc], k_ref[g]) + carry_out[:, e * S5_CW:(e + 1) * S5_CW]
            y_ref[:, gc] = y.astype(BF16)


def _s5(u_chunks, kmat, bmat, cmat, a_pow):
    n_chunks = u_chunks.shape[0]
    bw = S5_GB * S5_CW
    sw = S5_PB * 2 * S5_N
    return pl.pallas_call(
        functools.partial(_s5_kernel, n_chunks=n_chunks),
        grid=(S5_G // S5_GB,),
        in_specs=[pl.BlockSpec((n_chunks, bw), lambda i: (0, i)),
                  pl.BlockSpec((S5_GB, S5_CW, S5_CW), lambda i: (i, 0, 0)),
                  pl.BlockSpec((S5_PB, 2 * S5_CW, 8 * S5_N), lambda i: (i, 0, 0)),
                  pl.BlockSpec((S5_PB, 8 * S5_N, 2 * S5_CW), lambda i: (i, 0, 0)),
                  pl.BlockSpec((1, 4, sw), lambda i: (i, 0, 0))],
        out_specs=pl.BlockSpec((n_chunks, bw), lambda i: (0, i)),
        out_shape=jax.ShapeDtypeStruct(u_chunks.shape, BF16),
        scratch_shapes=[pltpu.VMEM((n_chunks, sw), F32)] * 4,
        compiler_params=_cparams("parallel"),
        name="s5_scan",
    )(u_chunks, kmat, bmat, cmat, a_pow)


def _row_keep(shape, tile_rows):
    row = pl.program_id(0) * tile_rows + lax.broadcasted_iota(jnp.int32, shape, 0)
    return row >= N_DUMMY


def _even_out_kernel(o_ref, ga_ref, y_ref, gb_ref, h_ref, wglu_ref, bglu_ref, wout_ref, gpost_ref, hout_ref):
    ya = o_ref[...].astype(F32) * _silu(ga_ref[...].astype(F32))
    yb = _gelu_tanh(y_ref[...].astype(F32))
    yb = yb * _sigmoid(_dot(yb.astype(BF16), wglu_ref[...]) + bglu_ref[...])
    yb = yb * _silu(gb_ref[...].astype(F32))
    out = _dot(ya.astype(BF16), wout_ref[0:ATTN_W, :]) + _dot(yb.astype(BF16), wout_ref[ATTN_W:, :])
    hnew = h_ref[...] + _rms(out, gpost_ref[...])
    hout_ref[...] = jnp.where(_row_keep(hnew.shape, TM), hnew, 0.0)


def _even_out(o, ga, y, gb, hres, wglu, bglu, wout, gpost):
    lp = hres.shape[0]
    row = lambda w: pl.BlockSpec((TM, w), lambda i: (i, 0))
    return pl.pallas_call(
        _even_out_kernel,
        grid=(lp // TM,),
        in_specs=[row(ATTN_W), row(ATTN_W), row(S5_W), row(S5_W), row(D_MODEL),
                  _full((S5_W, S5_W)), _full((1, S5_W)), _full((ATTN_W + S5_W, D_MODEL)), _full((1, D_MODEL))],
        out_specs=row(D_MODEL),
        out_shape=jax.ShapeDtypeStruct((lp, D_MODEL), F32),
        compiler_params=_cparams("parallel"),
        name="even_out",
    )(o, ga, y, gb, hres, wglu, bglu, wout, gpost)


def _odd_in_kernel(h_ref, gpre_ref, w_ref, xm_ref, z_ref):
    hn = _rms(h_ref[...], gpre_ref[...]).astype(BF16)
    xm_ref[...] = _dot(hn, w_ref[:, 0:ML_W]).astype(BF16)
    z_ref[...] = _dot(hn, w_ref[:, ML_W:]).astype(BF16)


def _odd_in(hres, gpre, w_in):
    lp = hres.shape[0]
    row = lambda w: pl.BlockSpec((TM, w), lambda i: (i, 0))
    return pl.pallas_call(
        _odd_in_kernel,
        grid=(lp // TM,),
        in_specs=[row(D_MODEL), _full((1, D_MODEL)), _full((D_MODEL, 2 * ML_W))],
        out_specs=[row(ML_W), row(ML_W)],
        out_shape=[jax.ShapeDtypeStruct((lp, ML_W), BF16)] * 2,
        compiler_params=_cparams("parallel"),
        name="odd_in",
    )(hres, gpre, w_in)


HALO = 8
GATE_LANES = 128


def _odd_mid_kernel(xm_ref, prev_ref, next_ref, cw_ref, cb_ref, wqk_ref, wv_ref, wg_ref, bg_ref,
                    q_ref, k_ref, v_ref, xc_ref, g_ref, ext):
    i = pl.program_id(0)
    n = pl.num_programs(0)
    xm = xm_ref[...]
    ext[0:HALO, :] = jnp.where(i > 0, prev_ref[...].astype(F32), 0.0)
    ext[HALO:HALO + TM, :] = xm.astype(F32)
    ext[HALO + TM:, :] = jnp.where(i < n - 1, next_ref[...].astype(F32), 0.0)
    conv = cb_ref[...]
    for j in range(CONV_W):
        conv = conv + cw_ref[j:j + 1, :] * ext[pl.ds(HALO - CONV_W // 2 + j, TM), :]
    xc = _silu(conv)
    xcb = xc.astype(BF16)
    xc_ref[...] = xcb

    for h in range(ML_H):
        cin = slice(h * ML_DV, (h + 1) * ML_DV)
        qk = _dot(xcb[:, cin], wqk_ref[h])
        q_ref[:, h * ML_DK:(h + 1) * ML_DK] = (qk[:, 0:ML_DK] * (ML_DK ** -0.5)).astype(BF16)
        k_ref[:, h * ML_DK:(h + 1) * ML_DK] = qk[:, ML_DK:].astype(BF16)
        v_ref[:, cin] = _dot(xm[:, cin], wv_ref[h]).astype(BF16)

    pre = _dot(xcb, wg_ref[...]) + bg_ref[...]
    lane = lax.broadcasted_iota(jnp.int32, pre.shape, 1)
    keep = _row_keep(pre.shape, TM)
    li = jnp.where(keep, pre, NEG_BIG)
    lf = jnp.where(keep, jnp.minimum(pre, 0.0) - jnp.log(1.0 + jnp.exp(-jnp.abs(pre))), 0.0)
    t = lax.broadcasted_iota(jnp.int32, (TM, TM), 0)
    s = lax.broadcasted_iota(jnp.int32, (TM, TM), 1)
    same = (t // ML_CHUNK) == (s // ML_CHUNK)
    tri_f = jnp.where(same & (s <= t), 1.0, 0.0).astype(BF16)
    tri_b = jnp.where(same & (s >= t), 1.0, 0.0).astype(BF16)
    lf_hi = lf.astype(BF16)
    lf_lo = (lf - lf_hi.astype(F32)).astype(BF16)
    cum_f = _dot(tri_f, lf_hi) + _dot(tri_f, lf_lo)
    cum_b = _dot(tri_b, lf_hi) + _dot(tri_b, lf_lo)
    g_ref[...] = jnp.where(lane < 2 * ML_H, li, jnp.where(lane < 3 * ML_H, cum_f, cum_b))


def _odd_mid(xm, cw, cb, wqk, wv, wg, bg):
    lp = xm.shape[0]
    row = lambda w: pl.BlockSpec((TM, w), lambda i: (i, 0))
    per = TM // HALO
    last = lp // HALO - 1
    return pl.pallas_call(
        _odd_mid_kernel,
        grid=(lp // TM,),
        in_specs=[row(ML_W),
                  pl.BlockSpec((HALO, ML_W), lambda i: (jnp.maximum(i * per - 1, 0), 0)),
                  pl.BlockSpec((HALO, ML_W), lambda i: (jnp.minimum((i + 1) * per, last), 0)),
                  _full((CONV_W, ML_W)), _full((1, ML_W)),
                  _full((ML_H, ML_DV, 2 * ML_DK)), _full((ML_H, ML_DV, ML_DV)),
                  _full((ML_W, GATE_LANES)), _full((1, GATE_LANES))],
        out_specs=[row(ML_H * ML_DK), row(ML_H * ML_DK), row(ML_W), row(ML_W), row(GATE_LANES)],
        out_shape=[jax.ShapeDtypeStruct((lp, ML_H * ML_DK), BF16), jax.ShapeDtypeStruct((lp, ML_H * ML_DK), BF16),
                   jax.ShapeDtypeStruct((lp, ML_W), BF16), jax.ShapeDtypeStruct((lp, ML_W), BF16),
                   jax.ShapeDtypeStruct((lp, GATE_LANES), F32)],
        scratch_shapes=[pltpu.VMEM((TM + 2 * HALO, ML_W), F32)],
        compiler_params=_cparams("parallel"),
        name="odd_mid",
    )(xm, xm, xm, cw, cb, wqk, wv, wg, bg)


def _mlstm_chunk(q, k, v, g, gt, h, backward, c_sc, n_sc, m_sc, slot):
    c = ML_CHUNK
    off = ML_H if backward else 0
    li_col = g[:, off + h:off + h + 1]
    li_row = gt[off + h:off + h + 1, :]
    b_col = g[:, 2 * ML_H + off + h:2 * ML_H + off + h + 1]
    b_row = gt[2 * ML_H + off + h:2 * ML_H + off + h + 1, :]
    b_last = b_col[0:1, :] if backward else b_col[c - 1:c, :]
    m_prev = m_sc[slot][:, 0:1]
    ct_prev = c_sc[slot]
    n_prev = n_sc[slot]

    t = lax.broadcasted_iota(jnp.int32, (c, c), 0)
    s = lax.broadcasted_iota(jnp.int32, (c, c), 1)
    allowed = (s >= t) if backward else (s <= t)
    d = jnp.where(allowed, b_col - b_row + li_row, NEG)
    inter_log = b_col + m_prev
    m_t = jnp.maximum(jnp.max(d, axis=-1, keepdims=True), inter_log)
    sw = jnp.exp(d - m_t) * _dot_nt(q, k)
    inter = jnp.exp(inter_log - m_t)
    num = _dot(sw.astype(BF16), v) + inter * _dot(q, ct_prev.astype(BF16))
    den = jnp.sum(sw, axis=-1, keepdims=True) + inter * jnp.sum(q.astype(F32) * n_prev, axis=-1, keepdims=True)
    out = num / jnp.maximum(jnp.abs(den), jnp.exp(-m_t))

    gk_col = b_last - b_col + li_col
    gk_row = b_last - b_row + li_row
    m_new = jnp.maximum(b_last + m_prev, jnp.max(gk_row, axis=-1, keepdims=True))
    decay = jnp.exp(b_last + m_prev - m_new)
    kw = k.astype(F32) * jnp.exp(gk_col - m_new)
    c_sc[slot] = decay * ct_prev + _dot_tn(kw.astype(BF16), v)
    n_sc[slot] = decay * n_prev + jnp.sum(kw, axis=0, keepdims=True)
    m_sc[slot] = jnp.broadcast_to(m_new, m_sc.shape[1:])
    return out


def _mlstm_kernel(qf_ref, kf_ref, vf_ref, gf_ref, qb_ref, kb_ref, vb_ref, gb_ref, hf_ref, hb_ref,
                  c_sc, n_sc, m_sc):
    @pl.when(pl.program_id(0) == 0)
    def _():
        c_sc[...] = jnp.zeros(c_sc.shape, F32)
        n_sc[...] = jnp.zeros(n_sc.shape, F32)
        m_sc[...] = jnp.zeros(m_sc.shape, F32)

    for backward, (q_ref, k_ref, v_ref, g_ref, h_ref) in enumerate(
            ((qf_ref, kf_ref, vf_ref, gf_ref, hf_ref), (qb_ref, kb_ref, vb_ref, gb_ref, hb_ref))):
        g = g_ref[...]
        gt = g.T
        for h in range(ML_H):
            out = _mlstm_chunk(q_ref[:, h * ML_DK:(h + 1) * ML_DK], k_ref[:, h * ML_DK:(h + 1) * ML_DK],
                               v_ref[:, h * ML_DV:(h + 1) * ML_DV], g, gt, h, bool(backward),
                               c_sc, n_sc, m_sc, backward * ML_H + h)
            h_ref[:, h * ML_DV:(h + 1) * ML_DV] = out.astype(BF16)


def _mlstm(q, k, v, gates):
    lp = q.shape[0]
    nch = lp // ML_CHUNK
    fwd = lambda w: pl.BlockSpec((ML_CHUNK, w), lambda i: (i, 0))
    bwd = lambda w: pl.BlockSpec((ML_CHUNK, w), lambda i: (nch - 1 - i, 0))
    qk_w = ML_H * ML_DK
    return pl.pallas_call(
        _mlstm_kernel,
        grid=(nch,),
        in_specs=[fwd(qk_w), fwd(qk_w), fwd(ML_W), fwd(GATE_LANES),
                  bwd(qk_w), bwd(qk_w), bwd(ML_W), bwd(GATE_LANES)],
        out_specs=[fwd(ML_W), bwd(ML_W)],
        out_shape=[jax.ShapeDtypeStruct((lp, ML_W), BF16)] * 2,
        scratch_shapes=[pltpu.VMEM((2 * ML_H, ML_DK, ML_DV), F32), pltpu.VMEM((2 * ML_H, 1, ML_DK), F32),
                        pltpu.VMEM((2 * ML_H, 1, 128), F32)],
        compiler_params=_cparams("arbitrary"),
        name="mlstm",
    )(q, k, v, gates, q, k, v, gates)


def _odd_out_kernel(hf_ref, hb_ref, xc_ref, z_ref, h_ref, wo_ref, bo_ref, hn_ref, sk_ref, wout_ref, gpost_ref,
                    hout_ref):
    xcb = xc_ref[...]
    xc = xcb.astype(F32)
    og = _sigmoid(_dot(xcb, wo_ref[...]) + bo_ref[...])
    cell = hf_ref[...].astype(F32) + hb_ref[...].astype(F32)
    parts = []
    for h in range(ML_H):
        sl = slice(h * ML_DV, (h + 1) * ML_DV)
        parts.append(_rms(cell[:, sl], hn_ref[:, sl]))
    cell = jnp.concatenate(parts, axis=1)
    hh = (og * cell + sk_ref[...] * xc) * _silu(z_ref[...].astype(F32))
    out = _dot(hh.astype(BF16), wout_ref[...])
    hnew = h_ref[...] + _rms(out, gpost_ref[...])
    hout_ref[...] = jnp.where(_row_keep(hnew.shape, TM), hnew, 0.0)


def _odd_out(hf, hb, xc, z, hres, wo, bo, hn, sk, wout, gpost):
    lp = hres.shape[0]
    row = lambda w: pl.BlockSpec((TM, w), lambda i: (i, 0))
    return pl.pallas_call(
        _odd_out_kernel,
        grid=(lp // TM,),
        in_specs=[row(ML_W), row(ML_W), row(ML_W), row(ML_W), row(D_MODEL),
                  _full((ML_W, ML_W)), _full((1, ML_W)), _full((1, ML_W)), _full((1, ML_W)),
                  _full((ML_W, D_MODEL)), _full((1, D_MODEL))],
        out_specs=row(D_MODEL),
        out_shape=jax.ShapeDtypeStruct((lp, D_MODEL), F32),
        compiler_params=_cparams("parallel"),
        name="odd_out",
    )(hf, hb, xc, z, hres, wo, bo, hn, sk, wout, gpost)


def _rope_tables(n_real):
    n = jnp.arange(n_real)
    zeros = jnp.zeros((FRONT,), F32)
    row = jnp.concatenate([zeros, (n // GRID_W).astype(F32)])
    col = jnp.concatenate([zeros, (n % GRID_W).astype(F32)])
    freqs = ROPE_THETA ** (-jnp.arange(0, ROPE_AXIS, 2, dtype=F32) / ROPE_AXIS)
    ang_r = row[:, None] * freqs[None]
    ang_c = col[:, None] * freqs[None]
    cos = jnp.concatenate([jnp.cos(ang_r)] * 2 + [jnp.cos(ang_c)] * 2, axis=1)
    sin = jnp.concatenate([-jnp.sin(ang_r), jnp.sin(ang_r), -jnp.sin(ang_c), jnp.sin(ang_c)], axis=1)
    return cos, sin


def kernel(x, meta_tokens, norm_pre, norm_post, w_in_even, q_norm, k_norm, ssm_a_re, ssm_a_im, ssm_log_dt, ssm_b_re, ssm_b_im, ssm_c_re, ssm_c_im, ssm_d, w_glu, b_glu, w_out_even, w_in_odd, conv_w, conv_b, w_q, w_k, w_v, w_igate, b_igate, w_fgate, b_fgate, w_ogate, b_ogate, head_norm, mlstm_skip, w_out_odd):
    bsz, n_real, _ = x.shape
    assert bsz == 1 and n_real % TK == 0 and (n_real + FRONT) % TM == 0
    lp = n_real + FRONT
    depth = norm_pre.shape[0]
    hres = jnp.concatenate([jnp.zeros((N_DUMMY, D_MODEL), F32), meta_tokens.astype(F32), x[0]], axis=0)
    cos, sin = _rope_tables(n_real)
    n_chunks = lp // S5_T
    n_chunks_pad = -(-n_chunks // 16) * 16
    row2 = lambda a: a.reshape(1, -1)

    for layer in range(depth):
        i = layer // 2
        gpre, gpost = row2(norm_pre[layer]), row2(norm_post[layer])
        if layer % 2 == 0:
            q, k, v, ga, u, gb = _even_in(hres, gpre, w_in_even[i].astype(BF16),
                                          row2(q_norm[i]) * (HEAD_DIM ** -0.5), row2(k_norm[i]), cos, sin)
            o = _attention(q, k, v)
            kmat, bmat, cmat, a_pow = _s5_tables(ssm_a_re[i], ssm_a_im[i], ssm_log_dt[i], ssm_b_re[i], ssm_b_im[i],
                                                 ssm_c_re[i], ssm_c_im[i], ssm_d[i])
            uc = u.reshape(n_chunks, S5_T, S5_G, S5_P).transpose(0, 2, 1, 3).reshape(n_chunks, S5_G * S5_CW)
            uc = jnp.pad(uc, ((0, n_chunks_pad - n_chunks), (0, 0)))
            yc = _s5(uc, kmat, bmat, cmat, a_pow)[:n_chunks]
            y = yc.reshape(n_chunks, S5_G, S5_T, S5_P).transpose(0, 2, 1, 3).reshape(lp, S5_W)
            hres = _even_out(o, ga, y, gb, hres, w_glu[i].astype(BF16), row2(b_glu[i]),
                             w_out_even[i].astype(BF16), gpost)
        else:
            xm, z = _odd_in(hres, gpre, w_in_odd[i].astype(BF16))
            wqk = jnp.concatenate([w_q[i], w_k[i]], axis=2).astype(BF16)
            wg = jnp.concatenate([w_igate[i, 0], w_igate[i, 1], w_fgate[i, 0], w_fgate[i, 1]], axis=1)
            wg = jnp.pad(wg, ((0, 0), (0, GATE_LANES - 4 * ML_H))).astype(BF16)
            bg = jnp.concatenate([b_igate[i, 0], b_igate[i, 1], b_fgate[i, 0], b_fgate[i, 1]])
            bg = jnp.pad(bg, (0, GATE_LANES - 4 * ML_H)).reshape(1, GATE_LANES)
            qm, km, vm, xc, gates = _odd_mid(xm, conv_w[i], row2(conv_b[i]), wqk, w_v[i].astype(BF16), wg, bg)
            hf, hb = _mlstm(qm, km, vm, gates)
            hres = _odd_out(hf, hb, xc, z, hres, w_ogate[i].astype(BF16), row2(b_ogate[i]), row2(head_norm[i]),
                            row2(mlstm_skip[i]), w_out_odd[i].astype(BF16), gpost)
    return hres[FRONT:].reshape(1, n_real, D_MODEL)
```

```python
import functools
import math

import jax
import jax.numpy as jnp
from jax import lax
from jax.experimental import pallas as pl
from jax.experimental.pallas import tpu as pltpu

F32 = jnp.float32
BF16 = jnp.bfloat16

D_MODEL = 1024
N_META = 16
FRONT = 128
N_DUMMY = FRONT - N_META
GRID_W = 64
EPS = 1e-6
NEG = -1e30

HEADS = 8
KV_HEADS = 2
KV_REP = HEADS // KV_HEADS
HEAD_DIM = 128
ROPE_AXIS = HEAD_DIM // 2
ROPE_THETA = 10000.0
ATTN_W = HEADS * HEAD_DIM
KV_W = KV_HEADS * HEAD_DIM
TQ = 384
TQS = 128
TK = 1024
VT_ROWS = HEAD_DIM + 16
LOGIT_SAFE = 60.0

S5_W = 1024
S5_P = 16
S5_G = S5_W // S5_P
S5_N = 64
S5_T = 16
S5_GB = 8
S5_PB = S5_GB // 2
S5_CW = S5_T * S5_P

ML_W = 2048
ML_H = 8
ML_DV = 256
ML_DK = 128
CONV_W = 5
ML_CHUNK = 128
NEG_BIG = -1e9

TM = 384
VMEM_LIMIT = 56 * 1024 * 1024


def _cparams(*sem):
    return pltpu.CompilerParams(dimension_semantics=sem, vmem_limit_bytes=VMEM_LIMIT)


def _rms(x, g):
    return x * lax.rsqrt(jnp.mean(x * x, axis=-1, keepdims=True) + EPS) * g


def _sigmoid(x):
    return 1.0 / (1.0 + jnp.exp(-x))


def _silu(x):
    return x * _sigmoid(x)


def _gelu_tanh(x):
    return 0.5 * x * (1.0 + jnp.tanh(math.sqrt(2.0 / math.pi) * (x + 0.044715 * (x * x * x))))


def _dot(a, b):
    return jnp.dot(a, b, preferred_element_type=F32)


def _dot_nt(a, b):
    return lax.dot_general(a, b, (((1,), (1,)), ((), ())), preferred_element_type=F32)


def _dot_tn(a, b):
    return lax.dot_general(a, b, (((0,), (0,)), ((), ())), preferred_element_type=F32)


def _full(shape):
    return pl.BlockSpec(shape, lambda *_: (0,) * len(shape))


def _even_in_kernel(h_ref, gpre_ref, w_ref, wvt_ref, qg_ref, kg_ref, cos_ref, sin_ref,
                    q_ref, k_ref, vt_ref, ga_ref, u_ref, gb_ref):
    hn = _rms(h_ref[...], gpre_ref[...]).astype(BF16)
    cos = cos_ref[...]
    sin = sin_ref[...]
    lane = lax.broadcasted_iota(jnp.int32, cos.shape, 1)
    first_half = (lane % ROPE_AXIS) < (ROPE_AXIS // 2)

    def head(x, g):
        y = _rms(x, g)
        partner = jnp.where(first_half, pltpu.roll(y, HEAD_DIM - ROPE_AXIS // 2, 1),
                            pltpu.roll(y, ROPE_AXIS // 2, 1))
        return (y * cos + partner * sin).astype(BF16)

    qsec = _dot(hn, w_ref[:, 0:ATTN_W])
    for h in range(HEADS):
        sl = slice(h * HEAD_DIM, (h + 1) * HEAD_DIM)
        q_ref[:, sl] = head(qsec[:, sl], qg_ref[...])
    ksec = _dot(hn, w_ref[:, ATTN_W:ATTN_W + KV_W])
    for h in range(KV_HEADS):
        sl = slice(h * HEAD_DIM, (h + 1) * HEAD_DIM)
        k_ref[:, sl] = head(ksec[:, sl], kg_ref[...])
    vt = _dot_nt(wvt_ref[...], hn).astype(BF16)
    for g in range(KV_HEADS):
        vt_ref[g * VT_ROWS:g * VT_ROWS + HEAD_DIM, :] = vt[g * HEAD_DIM:(g + 1) * HEAD_DIM]
        vt_ref[g * VT_ROWS + HEAD_DIM:(g + 1) * VT_ROWS, :] = jnp.ones((VT_ROWS - HEAD_DIM, TM), BF16)
    c0 = ATTN_W + 2 * KV_W
    ga_ref[...] = _dot(hn, w_ref[:, c0:c0 + ATTN_W]).astype(BF16)
    c0 += ATTN_W
    u_ref[...] = _dot(hn, w_ref[:, c0:c0 + S5_W])
    c0 += S5_W
    gb_ref[...] = _dot(hn, w_ref[:, c0:c0 + S5_W]).astype(BF16)


def _even_in(hres, gpre, w_in, wvt, qg, kg, cos, sin):
    lp = hres.shape[0]
    win_w = w_in.shape[1]
    row = lambda w: pl.BlockSpec((TM, w), lambda i: (i, 0))
    vt_rows = KV_HEADS * VT_ROWS
    col = pl.BlockSpec((vt_rows, TM), lambda i: (0, i))
    widths = [ATTN_W, KV_W, None, ATTN_W, S5_W, S5_W]
    return pl.pallas_call(
        _even_in_kernel,
        grid=(lp // TM,),
        in_specs=[row(D_MODEL), _full((1, D_MODEL)), _full((D_MODEL, win_w)), _full((KV_W, D_MODEL)),
                  _full((1, HEAD_DIM)), _full((1, HEAD_DIM)), row(HEAD_DIM), row(HEAD_DIM)],
        out_specs=[col if w is None else row(w) for w in widths],
        out_shape=[jax.ShapeDtypeStruct((vt_rows, lp) if w is None else (lp, w), F32 if n == 4 else BF16)
                   for n, w in enumerate(widths)],
        compiler_params=_cparams("parallel"),
        name="even_in",
    )(hres, gpre, w_in, wvt, qg, kg, cos, sin)


def _attn_kernel(q_ref, k_ref, vt_ref, o_ref, m_sc, acc_sc, p_sc, *, n_kv_tiles, stabilised):
    qs = [jnp.concatenate([q_ref[a * TQS:(a + 1) * TQS, r * HEAD_DIM:(r + 1) * HEAD_DIM] for r in range(KV_REP)],
                          axis=0) for a in range(TQ // TQS)]

    def scores(kb, a, mask_dummy=False):
        st = _dot_nt(kb, qs[a])
        if mask_dummy:
            key = lax.broadcasted_iota(jnp.int32, st.shape, 0)
            st = jnp.where(key >= N_DUMMY, st, NEG)
        return st

    def tile_start(j):
        return pl.multiple_of(FRONT + j * TK, 128)

    k0, vt0 = k_ref[0:FRONT, :], vt_ref[:, 0:FRONT]
    if stabilised:
        def update(kb, vtb, mask_dummy):
            for a in range(len(qs)):
                st = scores(kb, a, mask_dummy)
                m_old = m_sc[a]
                m_new = jnp.maximum(m_old, jnp.max(st, axis=0, keepdims=True))
                p = jnp.exp2(st - m_new).astype(BF16)
                acc_sc[a] = jnp.exp2(m_old - m_new) * acc_sc[a] + _dot(vtb, p)
                m_sc[a] = m_new

        m_sc[...] = jnp.full(m_sc.shape, NEG, F32)
        acc_sc[...] = jnp.zeros(acc_sc.shape, F32)
        update(k0, vt0, True)

        def body(j, carry):
            update(k_ref[pl.ds(tile_start(j), TK), :], vt_ref[:, pl.ds(tile_start(j), TK)], False)
            return carry

        lax.fori_loop(0, n_kv_tiles, body, 0)
    else:
        for a in range(len(qs)):
            acc_sc[a] = _dot(vt0, jnp.exp2(scores(k0, a, True).astype(BF16)))
            p_sc[0, a] = jnp.exp2(scores(k_ref[FRONT:FRONT + TK, :], a).astype(BF16))

        def body(j, carry):
            cur, prev = j % 2, (j + 1) % 2
            kb = k_ref[pl.ds(tile_start(j), TK), :]
            vtb = vt_ref[:, pl.ds(tile_start(j - 1), TK)]
            sts = [scores(kb, a) for a in range(len(qs))]
            for a in range(len(qs)):
                acc_sc[a] += _dot(vtb, p_sc[prev, a])
            for a in range(len(qs)):
                p_sc[cur, a] = jnp.exp2(sts[a].astype(BF16))
            return carry

        lax.fori_loop(1, n_kv_tiles, body, 0)
        last = (n_kv_tiles - 1) % 2
        vtb = vt_ref[:, FRONT + (n_kv_tiles - 1) * TK:FRONT + n_kv_tiles * TK]
        for a in range(len(qs)):
            acc_sc[a] += _dot(vtb, p_sc[last, a])

    for a in range(TQ // TQS):
        ot = acc_sc[a, 0:HEAD_DIM, :] / acc_sc[a, HEAD_DIM:HEAD_DIM + 1, :]
        for r in range(KV_REP):
            o_ref[a * TQS:(a + 1) * TQS, r * HEAD_DIM:(r + 1) * HEAD_DIM] = (
                ot[:, r * TQS:(r + 1) * TQS].T.astype(BF16))


def _attention(q, k, vt, stabilised):
    lp = q.shape[0]
    n_kv_tiles = (lp - FRONT) // TK
    gw = KV_REP * HEAD_DIM
    n_sub = TQ // TQS
    return pl.pallas_call(
        functools.partial(_attn_kernel, n_kv_tiles=n_kv_tiles, stabilised=stabilised),
        grid=(KV_HEADS, lp // TQ),
        in_specs=[pl.BlockSpec((TQ, gw), lambda g, i: (i, g)),
                  pl.BlockSpec((lp, HEAD_DIM), lambda g, i: (0, g)),
                  pl.BlockSpec((VT_ROWS, lp), lambda g, i: (g, 0))],
        out_specs=pl.BlockSpec((TQ, gw), lambda g, i: (i, g)),
        out_shape=jax.ShapeDtypeStruct((lp, ATTN_W), BF16),
        scratch_shapes=[pltpu.VMEM((n_sub, 1, KV_REP * TQS), F32),
                        pltpu.VMEM((n_sub, VT_ROWS, KV_REP * TQS), F32),
                        pltpu.VMEM((2, n_sub, TK, KV_REP * TQS), BF16)],
        compiler_params=_cparams("parallel", "parallel"),
        name="attention_stabilised" if stabilised else "attention",
    )(q, k, vt)


def _s5_tables(a_re, a_im, log_dt, b_re, b_im, c_re, c_im, d_skip):
    t = S5_T
    lag = jnp.arange(t + 1, dtype=F32)
    per_dir = []
    for d in range(2):
        a = lax.complex(a_re[d], a_im[d])
        adt = a * jnp.exp(log_dt[d])[:, None]
        a_bar = jnp.exp(adt)
        b_bar = ((a_bar - 1.0) / a)[..., None] * lax.complex(b_re[d], b_im[d])
        c = lax.complex(c_re[d], c_im[d])
        pw = jnp.exp(adt[None] * lag[:, None, None].astype(jnp.complex64))
        klag = jnp.real(jnp.einsum('gon,dgn,gni->dgoi', c, pw[:t], b_bar,
                                   precision=lax.Precision.HIGHEST))
        per_dir.append((b_bar, c, pw, klag))
    (bb0, c0, pw0, kl0), (bb1, c1, pw1, kl1) = per_dir
    ii = jnp.arange(t)[:, None, None]
    jj = jnp.arange(t)[None, :, None]
    dd = jnp.arange(t)[None, None, :]
    place_f = (jj - ii == dd).astype(F32)
    place_b = (ii - jj == dd).astype(F32)
    skip = d_skip.reshape(S5_G, S5_P)[:, :, None] * jnp.eye(S5_P, dtype=F32)
    kl0 = kl0.at[0].add(skip)
    kmat = (jnp.einsum('ijd,dgop->gipjo', place_f, kl0, precision=lax.Precision.HIGHEST)
            + jnp.einsum('ijd,dgop->gipjo', place_b, kl1, precision=lax.Precision.HIGHEST)
            ).reshape(S5_G, S5_CW, S5_CW)

    bf = pw0[:t][::-1][..., None] * bb0[None]
    bb = pw1[:t][..., None] * bb1[None]
    cf = c0[None] * pw0[1:t + 1][:, :, None, :]
    cb = c1[None] * pw1[1:t + 1][::-1][:, :, None, :]

    def in_mat(x):
        return x.transpose(1, 0, 3, 2).reshape(S5_G, S5_CW, S5_N)

    def out_mat(x):
        return x.transpose(1, 3, 0, 2).reshape(S5_G, S5_N, S5_CW)

    n_pair = S5_G // 2
    zero_in = jnp.zeros((n_pair, S5_CW, S5_N), F32)
    zero_out = jnp.zeros((n_pair, S5_N, S5_CW), F32)

    def pair_in(x):
        e, o = x[0::2], x[1::2]
        return jnp.concatenate([jnp.concatenate([e, zero_in], axis=2),
                                jnp.concatenate([zero_in, o], axis=2)], axis=1)

    def pair_out(x):
        e, o = x[0::2], x[1::2]
        return jnp.concatenate([jnp.concatenate([e, zero_out], axis=2),
                                jnp.concatenate([zero_out, o], axis=2)], axis=1)

    bmat = jnp.concatenate([pair_in(in_mat(jnp.real(bf))), pair_in(in_mat(jnp.imag(bf))),
                            pair_in(in_mat(jnp.real(bb))), pair_in(in_mat(jnp.imag(bb)))], axis=2)
    cmat = jnp.concatenate([pair_out(out_mat(jnp.real(cf))), pair_out(out_mat(-jnp.imag(cf))),
                            pair_out(out_mat(jnp.real(cb))), pair_out(out_mat(-jnp.imag(cb)))], axis=1)

    def pair_row(x):
        return x.reshape(S5_G // S5_GB, 1, S5_PB * 2 * S5_N)

    a_pow = jnp.concatenate([pair_row(jnp.real(pw0[t])), pair_row(jnp.imag(pw0[t])),
                             pair_row(jnp.real(pw1[t])), pair_row(jnp.imag(pw1[t]))], axis=1)
    return kmat.astype(BF16), bmat.astype(BF16), cmat.astype(BF16), a_pow


LANE_GROUPS = 128 // S5_P


def _swap_token_group(xs):
    block = lax.broadcasted_iota(jnp.int32, xs[0].shape, 1) // S5_P
    xs = list(xs)
    d = LANE_GROUPS // 2
    while d:
        upper = (block & d) != 0
        for a in range(LANE_GROUPS):
            if a & d:
                continue
            lo, hi = xs[a], xs[a + d]
            xs[a] = jnp.where(upper, pltpu.roll(hi, S5_P * d, 1), lo)
            xs[a + d] = jnp.where(upper, hi, pltpu.roll(lo, 128 - S5_P * d, 1))
        d //= 2
    return xs


def _s5_pack_kernel(u_ref, o_ref, *, n_chunks):
    n_main = n_chunks // 16 * 16
    for half in range(S5_T // LANE_GROUPS):
        xs = [u_ref[pl.ds(LANE_GROUPS * half + i, n_chunks, stride=S5_T), :] for i in range(LANE_GROUPS)]
        for g, y in enumerate(_swap_token_group(xs)):
            cols = slice(g * S5_CW + half * 128, g * S5_CW + (half + 1) * 128)
            o_ref[0:n_main, cols] = y[0:n_main].astype(BF16)
            if o_ref.shape[0] > n_main:
                tail = jnp.concatenate([y[n_main:], jnp.zeros((o_ref.shape[0] - n_chunks, 128), F32)], axis=0)
                o_ref[n_main:, cols] = tail.astype(BF16)


def _s5_unpack_kernel(y_ref, o_ref, *, n_chunks):
    for half in range(S5_T // LANE_GROUPS):
        ys = [y_ref[:, g * S5_CW + half * 128:g * S5_CW + (half + 1) * 128].astype(F32)[0:n_chunks]
              for g in range(LANE_GROUPS)]
        for i, x in enumerate(_swap_token_group(ys)):
            o_ref[pl.ds(LANE_GROUPS * half + i, n_chunks, stride=S5_T), :] = x


def _s5_pack(u, n_chunks_pad):
    lp = u.shape[0]
    bw = LANE_GROUPS * S5_CW
    return pl.pallas_call(
        functools.partial(_s5_pack_kernel, n_chunks=lp // S5_T),
        grid=(S5_W // 128,),
        in_specs=[pl.BlockSpec((lp, 128), lambda i: (0, i))],
        out_specs=pl.BlockSpec((n_chunks_pad, bw), lambda i: (0, i)),
        out_shape=jax.ShapeDtypeStruct((n_chunks_pad, S5_G * S5_CW), BF16),
        compiler_params=_cparams("parallel"),
        name="s5_pack",
    )(u)


def _s5_unpack(yc, lp):
    bw = LANE_GROUPS * S5_CW
    return pl.pallas_call(
        functools.partial(_s5_unpack_kernel, n_chunks=lp // S5_T),
        grid=(S5_W // 128,),
        in_specs=[pl.BlockSpec((yc.shape[0], bw), lambda i: (0, i))],
        out_specs=pl.BlockSpec((lp, 128), lambda i: (0, i)),
        out_shape=jax.ShapeDtypeStruct((lp, S5_W), F32),
        compiler_params=_cparams("parallel"),
        name="s5_unpack",
    )(yc)


def _s5_kernel(u_ref, k_ref, b_ref, c_ref, a_ref, y_ref, fre, fim, bre, bim, *, n_chunks):
    pw = 2 * S5_N
    for p in range(S5_PB):
        s = _dot(u_ref[:, p * 2 * S5_CW:(p + 1) * 2 * S5_CW], b_ref[p])
        cols = slice(p * pw, (p + 1) * pw)
        fre[:, cols] = s[:, 0:pw]
        fim[:, cols] = s[:, pw:2 * pw]
        bre[:, cols] = s[:, 2 * pw:3 * pw]
        bim[:, cols] = s[:, 3 * pw:4 * pw]

    far, fai, bar, bai = a_ref[0, 0:1, :], a_ref[0, 1:2, :], a_ref[0, 2:3, :], a_ref[0, 3:4, :]

    def step(i, carry):
        xr, xi, yr, yi = carry
        rf = pl.ds(i, 1)
        rb = pl.ds(n_chunks - 1 - i, 1)
        sr, si = fre[rf, :], fim[rf, :]
        tr, ti = bre[rb, :], bim[rb, :]
        fre[rf, :] = xr
        fim[rf, :] = xi
        bre[rb, :] = yr
        bim[rb, :] = yi
        return (far * xr - fai * xi + sr, far * xi + fai * xr + si,
                bar * yr - bai * yi + tr, bar * yi + bai * yr + ti)

    zero = jnp.zeros((1, S5_PB * pw), F32)
    lax.fori_loop(0, n_chunks, step, (zero, zero, zero, zero))

    for p in range(S5_PB):
        cols = slice(p * pw, (p + 1) * pw)
        state = jnp.concatenate([fre[:, cols], fim[:, cols], bre[:, cols], bim[:, cols]], axis=1).astype(BF16)
        carry_out = _dot(state, c_ref[p])
        for e in range(2):
            g = 2 * p + e
            gc = slice(g * S5_CW, (g + 1) * S5_CW)
            y = _dot(u_ref[:, gc], k_ref[g]) + carry_out[:, e * S5_CW:(e + 1) * S5_CW]
            y_ref[:, gc] = y.astype(BF16)


def _s5(u_chunks, kmat, bmat, cmat, a_pow):
    n_chunks = u_chunks.shape[0]
    bw = S5_GB * S5_CW
    sw = S5_PB * 2 * S5_N
    return pl.pallas_call(
        functools.partial(_s5_kernel, n_chunks=n_chunks),
        grid=(S5_G // S5_GB,),
        in_specs=[pl.BlockSpec((n_chunks, bw), lambda i: (0, i)),
                  pl.BlockSpec((S5_GB, S5_CW, S5_CW), lambda i: (i, 0, 0)),
                  pl.BlockSpec((S5_PB, 2 * S5_CW, 8 * S5_N), lambda i: (i, 0, 0)),
                  pl.BlockSpec((S5_PB, 8 * S5_N, 2 * S5_CW), lambda i: (i, 0, 0)),
                  pl.BlockSpec((1, 4, sw), lambda i: (i, 0, 0))],
        out_specs=pl.BlockSpec((n_chunks, bw), lambda i: (0, i)),
        out_shape=jax.ShapeDtypeStruct(u_chunks.shape, BF16),
        scratch_shapes=[pltpu.VMEM((n_chunks, sw), F32)] * 4,
        compiler_params=_cparams("parallel"),
        name="s5_scan",
    )(u_chunks, kmat, bmat, cmat, a_pow)


def _row_keep(shape, tile_rows):
    row = pl.program_id(0) * tile_rows + lax.broadcasted_iota(jnp.int32, shape, 0)
    return row >= N_DUMMY


def _even_out_kernel(o_ref, ga_ref, y_ref, gb_ref, h_ref, wglu_ref, bglu_ref, wout_ref, gpost_ref, hout_ref):
    ya = o_ref[...].astype(F32) * _silu(ga_ref[...].astype(F32))
    yb = _gelu_tanh(y_ref[...].astype(F32))
    yb = yb * _sigmoid(_dot(yb.astype(BF16), wglu_ref[...]) + bglu_ref[...])
    yb = yb * _silu(gb_ref[...].astype(F32))
    out = _dot(ya.astype(BF16), wout_ref[0:ATTN_W, :]) + _dot(yb.astype(BF16), wout_ref[ATTN_W:, :])
    hnew = h_ref[...] + _rms(out, gpost_ref[...])
    hout_ref[...] = jnp.where(_row_keep(hnew.shape, TM), hnew, 0.0)


def _even_out(o, ga, y, gb, hres, wglu, bglu, wout, gpost):
    lp = hres.shape[0]
    row = lambda w: pl.BlockSpec((TM, w), lambda i: (i, 0))
    return pl.pallas_call(
        _even_out_kernel,
        grid=(lp // TM,),
        in_specs=[row(ATTN_W), row(ATTN_W), row(S5_W), row(S5_W), row(D_MODEL),
                  _full((S5_W, S5_W)), _full((1, S5_W)), _full((ATTN_W + S5_W, D_MODEL)), _full((1, D_MODEL))],
        out_specs=row(D_MODEL),
        out_shape=jax.ShapeDtypeStruct((lp, D_MODEL), F32),
        compiler_params=_cparams("parallel"),
        name="even_out",
    )(o, ga, y, gb, hres, wglu, bglu, wout, gpost)


def _odd_in_kernel(h_ref, gpre_ref, w_ref, xm_ref, z_ref):
    hn = _rms(h_ref[...], gpre_ref[...]).astype(BF16)
    xm_ref[...] = _dot(hn, w_ref[:, 0:ML_W]).astype(BF16)
    z_ref[...] = _dot(hn, w_ref[:, ML_W:]).astype(BF16)


def _odd_in(hres, gpre, w_in):
    lp = hres.shape[0]
    row = lambda w: pl.BlockSpec((TM, w), lambda i: (i, 0))
    return pl.pallas_call(
        _odd_in_kernel,
        grid=(lp // TM,),
        in_specs=[row(D_MODEL), _full((1, D_MODEL)), _full((D_MODEL, 2 * ML_W))],
        out_specs=[row(ML_W), row(ML_W)],
        out_shape=[jax.ShapeDtypeStruct((lp, ML_W), BF16)] * 2,
        compiler_params=_cparams("parallel"),
        name="odd_in",
    )(hres, gpre, w_in)


HALO = 8
GATE_LANES = 128


def _odd_mid_kernel(xm_ref, prev_ref, next_ref, cw_ref, cb_ref, wqk_ref, wv_ref, wg_ref, bg_ref,
                    q_ref, k_ref, v_ref, xc_ref, g_ref, ext):
    i = pl.program_id(0)
    n = pl.num_programs(0)
    xm = xm_ref[...]
    ext[0:HALO, :] = jnp.where(i > 0, prev_ref[...].astype(F32), 0.0)
    ext[HALO:HALO + TM, :] = xm.astype(F32)
    ext[HALO + TM:, :] = jnp.where(i < n - 1, next_ref[...].astype(F32), 0.0)
    conv = cb_ref[...]
    for j in range(CONV_W):
        conv = conv + cw_ref[j:j + 1, :] * ext[pl.ds(HALO - CONV_W // 2 + j, TM), :]
    xc = _silu(conv)
    xcb = xc.astype(BF16)
    xc_ref[...] = xcb

    for h in range(ML_H):
        cin = slice(h * ML_DV, (h + 1) * ML_DV)
        qk = _dot(xcb[:, cin], wqk_ref[h])
        q_ref[:, h * ML_DK:(h + 1) * ML_DK] = (qk[:, 0:ML_DK] * (ML_DK ** -0.5)).astype(BF16)
        k_ref[:, h * ML_DK:(h + 1) * ML_DK] = qk[:, ML_DK:].astype(BF16)
        v_ref[:, cin] = _dot(xm[:, cin], wv_ref[h]).astype(BF16)

    pre = _dot(xcb, wg_ref[...]) + bg_ref[...]
    lane = lax.broadcasted_iota(jnp.int32, pre.shape, 1)
    keep = _row_keep(pre.shape, TM)
    li = jnp.where(keep, pre, NEG_BIG)
    lf = jnp.where(keep, jnp.minimum(pre, 0.0) - jnp.log(1.0 + jnp.exp(-jnp.abs(pre))), 0.0)
    t = lax.broadcasted_iota(jnp.int32, (TM, TM), 0)
    s = lax.broadcasted_iota(jnp.int32, (TM, TM), 1)
    same = (t // ML_CHUNK) == (s // ML_CHUNK)
    tri_f = jnp.where(same & (s <= t), 1.0, 0.0).astype(BF16)
    tri_b = jnp.where(same & (s >= t), 1.0, 0.0).astype(BF16)
    lf_hi = lf.astype(BF16)
    lf_lo = (lf - lf_hi.astype(F32)).astype(BF16)
    cum_f = _dot(tri_f, lf_hi) + _dot(tri_f, lf_lo)
    cum_b = _dot(tri_b, lf_hi) + _dot(tri_b, lf_lo)
    g_ref[...] = jnp.where(lane < 2 * ML_H, li, jnp.where(lane < 3 * ML_H, cum_f, cum_b))


def _odd_mid(xm, cw, cb, wqk, wv, wg, bg):
    lp = xm.shape[0]
    row = lambda w: pl.BlockSpec((TM, w), lambda i: (i, 0))
    per = TM // HALO
    last = lp // HALO - 1
    return pl.pallas_call(
        _odd_mid_kernel,
        grid=(lp // TM,),
        in_specs=[row(ML_W),
                  pl.BlockSpec((HALO, ML_W), lambda i: (jnp.maximum(i * per - 1, 0), 0)),
                  pl.BlockSpec((HALO, ML_W), lambda i: (jnp.minimum((i + 1) * per, last), 0)),
                  _full((CONV_W, ML_W)), _full((1, ML_W)),
                  _full((ML_H, ML_DV, 2 * ML_DK)), _full((ML_H, ML_DV, ML_DV)),
                  _full((ML_W, GATE_LANES)), _full((1, GATE_LANES))],
        out_specs=[row(ML_H * ML_DK), row(ML_H * ML_DK), row(ML_W), row(ML_W), row(GATE_LANES)],
        out_shape=[jax.ShapeDtypeStruct((lp, ML_H * ML_DK), BF16), jax.ShapeDtypeStruct((lp, ML_H * ML_DK), BF16),
                   jax.ShapeDtypeStruct((lp, ML_W), BF16), jax.ShapeDtypeStruct((lp, ML_W), BF16),
                   jax.ShapeDtypeStruct((lp, GATE_LANES), F32)],
        scratch_shapes=[pltpu.VMEM((TM + 2 * HALO, ML_W), F32)],
        compiler_params=_cparams("parallel"),
        name="odd_mid",
    )(xm, xm, xm, cw, cb, wqk, wv, wg, bg)


def _mlstm_chunk(q, k, v, g, gt, h, backward, c_sc, n_sc, m_sc, slot):
    c = ML_CHUNK
    off = ML_H if backward else 0
    li_col = g[:, off + h:off + h + 1]
    li_row = gt[off + h:off + h + 1, :]
    b_col = g[:, 2 * ML_H + off + h:2 * ML_H + off + h + 1]
    b_row = gt[2 * ML_H + off + h:2 * ML_H + off + h + 1, :]
    b_last = b_col[0:1, :] if backward else b_col[c - 1:c, :]
    m_prev = m_sc[slot][:, 0:1]
    ct_prev = c_sc[slot]
    n_prev = n_sc[slot]

    t = lax.broadcasted_iota(jnp.int32, (c, c), 0)
    s = lax.broadcasted_iota(jnp.int32, (c, c), 1)
    allowed = (s >= t) if backward else (s <= t)
    d = jnp.where(allowed, b_col - b_row + li_row, NEG)
    inter_log = b_col + m_prev
    m_t = jnp.maximum(jnp.max(d, axis=-1, keepdims=True), inter_log)
    sw = jnp.exp(d - m_t) * _dot_nt(q, k)
    inter = jnp.exp(inter_log - m_t)
    num = _dot(sw.astype(BF16), v) + inter * _dot(q, ct_prev.astype(BF16))
    den = jnp.sum(sw, axis=-1, keepdims=True) + inter * jnp.sum(q.astype(F32) * n_prev, axis=-1, keepdims=True)
    out = num / jnp.maximum(jnp.abs(den), jnp.exp(-m_t))

    gk_col = b_last - b_col + li_col
    gk_row = b_last - b_row + li_row
    m_new = jnp.maximum(b_last + m_prev, jnp.max(gk_row, axis=-1, keepdims=True))
    decay = jnp.exp(b_last + m_prev - m_new)
    kw = k.astype(F32) * jnp.exp(gk_col - m_new)
    c_sc[slot] = decay * ct_prev + _dot_tn(kw.astype(BF16), v)
    n_sc[slot] = decay * n_prev + jnp.sum(kw, axis=0, keepdims=True)
    m_sc[slot] = jnp.broadcast_to(m_new, m_sc.shape[1:])
    return out


def _mlstm_kernel(qf_ref, kf_ref, vf_ref, gf_ref, qb_ref, kb_ref, vb_ref, gb_ref, hf_ref, hb_ref,
                  c_sc, n_sc, m_sc):
    @pl.when(pl.program_id(0) == 0)
    def _():
        c_sc[...] = jnp.zeros(c_sc.shape, F32)
        n_sc[...] = jnp.zeros(n_sc.shape, F32)
        m_sc[...] = jnp.zeros(m_sc.shape, F32)

    for backward, (q_ref, k_ref, v_ref, g_ref, h_ref) in enumerate(
            ((qf_ref, kf_ref, vf_ref, gf_ref, hf_ref), (qb_ref, kb_ref, vb_ref, gb_ref, hb_ref))):
        g = g_ref[...]
        gt = g.T
        for h in range(ML_H):
            out = _mlstm_chunk(q_ref[:, h * ML_DK:(h + 1) * ML_DK], k_ref[:, h * ML_DK:(h + 1) * ML_DK],
                               v_ref[:, h * ML_DV:(h + 1) * ML_DV], g, gt, h, bool(backward),
                               c_sc, n_sc, m_sc, backward * ML_H + h)
            h_ref[:, h * ML_DV:(h + 1) * ML_DV] = out.astype(BF16)


def _mlstm(q, k, v, gates):
    lp = q.shape[0]
    nch = lp // ML_CHUNK
    fwd = lambda w: pl.BlockSpec((ML_CHUNK, w), lambda i: (i, 0))
    bwd = lambda w: pl.BlockSpec((ML_CHUNK, w), lambda i: (nch - 1 - i, 0))
    qk_w = ML_H * ML_DK
    return pl.pallas_call(
        _mlstm_kernel,
        grid=(nch,),
        in_specs=[fwd(qk_w), fwd(qk_w), fwd(ML_W), fwd(GATE_LANES),
                  bwd(qk_w), bwd(qk_w), bwd(ML_W), bwd(GATE_LANES)],
        out_specs=[fwd(ML_W), bwd(ML_W)],
        out_shape=[jax.ShapeDtypeStruct((lp, ML_W), BF16)] * 2,
        scratch_shapes=[pltpu.VMEM((2 * ML_H, ML_DK, ML_DV), F32), pltpu.VMEM((2 * ML_H, 1, ML_DK), F32),
                        pltpu.VMEM((2 * ML_H, 1, 128), F32)],
        compiler_params=_cparams("arbitrary"),
        name="mlstm",
    )(q, k, v, gates, q, k, v, gates)


def _odd_out_kernel(hf_ref, hb_ref, xc_ref, z_ref, h_ref, wo_ref, bo_ref, hn_ref, sk_ref, wout_ref, gpost_ref,
                    hout_ref):
    xcb = xc_ref[...]
    xc = xcb.astype(F32)
    og = _sigmoid(_dot(xcb, wo_ref[...]) + bo_ref[...])
    cell = hf_ref[...].astype(F32) + hb_ref[...].astype(F32)
    parts = []
    for h in range(ML_H):
        sl = slice(h * ML_DV, (h + 1) * ML_DV)
        parts.append(_rms(cell[:, sl], hn_ref[:, sl]))
    cell = jnp.concatenate(parts, axis=1)
    hh = (og * cell + sk_ref[...] * xc) * _silu(z_ref[...].astype(F32))
    out = _dot(hh.astype(BF16), wout_ref[...])
    hnew = h_ref[...] + _rms(out, gpost_ref[...])
    hout_ref[...] = jnp.where(_row_keep(hnew.shape, TM), hnew, 0.0)


def _odd_out(hf, hb, xc, z, hres, wo, bo, hn, sk, wout, gpost):
    lp = hres.shape[0]
    row = lambda w: pl.BlockSpec((TM, w), lambda i: (i, 0))
    return pl.pallas_call(
        _odd_out_kernel,
        grid=(lp // TM,),
        in_specs=[row(ML_W), row(ML_W), row(ML_W), row(ML_W), row(D_MODEL),
                  _full((ML_W, ML_W)), _full((1, ML_W)), _full((1, ML_W)), _full((1, ML_W)),
                  _full((ML_W, D_MODEL)), _full((1, D_MODEL))],
        out_specs=row(D_MODEL),
        out_shape=jax.ShapeDtypeStruct((lp, D_MODEL), F32),
        compiler_params=_cparams("parallel"),
        name="odd_out",
    )(hf, hb, xc, z, hres, wo, bo, hn, sk, wout, gpost)


def _rope_tables(n_real):
    n = jnp.arange(n_real)
    zeros = jnp.zeros((FRONT,), F32)
    row = jnp.concatenate([zeros, (n // GRID_W).astype(F32)])
    col = jnp.concatenate([zeros, (n % GRID_W).astype(F32)])
    freqs = ROPE_THETA ** (-jnp.arange(0, ROPE_AXIS, 2, dtype=F32) / ROPE_AXIS)
    ang_r = row[:, None] * freqs[None]
    ang_c = col[:, None] * freqs[None]
    cos = jnp.concatenate([jnp.cos(ang_r)] * 2 + [jnp.cos(ang_c)] * 2, axis=1)
    sin = jnp.concatenate([-jnp.sin(ang_r), jnp.sin(ang_r), -jnp.sin(ang_c), jnp.sin(ang_c)], axis=1)
    return cos, sin


def kernel(x, meta_tokens, norm_pre, norm_post, w_in_even, q_norm, k_norm, ssm_a_re, ssm_a_im, ssm_log_dt, ssm_b_re, ssm_b_im, ssm_c_re, ssm_c_im, ssm_d, w_glu, b_glu, w_out_even, w_in_odd, conv_w, conv_b, w_q, w_k, w_v, w_igate, b_igate, w_fgate, b_fgate, w_ogate, b_ogate, head_norm, mlstm_skip, w_out_odd):
    bsz, n_real, _ = x.shape
    assert bsz == 1 and n_real % TK == 0 and (n_real + FRONT) % TM == 0
    lp = n_real + FRONT
    depth = norm_pre.shape[0]
    hres = jnp.concatenate([jnp.zeros((N_DUMMY, D_MODEL), F32), meta_tokens.astype(F32), x[0]], axis=0)
    cos, sin = _rope_tables(n_real)
    n_chunks = lp // S5_T
    n_chunks_pad = -(-n_chunks // 16) * 16
    row2 = lambda a: a.reshape(1, -1)

    for layer in range(depth):
        i = layer // 2
        gpre, gpost = row2(norm_pre[layer]), row2(norm_post[layer])
        if layer % 2 == 0:
            w_in = w_in_even[i].astype(BF16)
            wvt = w_in[:, ATTN_W + KV_W:ATTN_W + 2 * KV_W].T
            qg = row2(q_norm[i]) * (HEAD_DIM ** -0.5 * math.log2(math.e))
            q, k, vt, ga, u, gb = _even_in(hres, gpre, w_in, wvt, qg, row2(k_norm[i]), cos, sin)
            logit_bound = (math.sqrt(HEAD_DIM) * math.log2(math.e)
                           * jnp.max(jnp.abs(q_norm[i])) * jnp.max(jnp.abs(k_norm[i])))
            o = lax.cond(logit_bound <= LOGIT_SAFE,
                         lambda q, k, vt: _attention(q, k, vt, False),
                         lambda q, k, vt: _attention(q, k, vt, True), q, k, vt)
            kmat, bmat, cmat, a_pow = _s5_tables(ssm_a_re[i], ssm_a_im[i], ssm_log_dt[i], ssm_b_re[i], ssm_b_im[i],
                                                 ssm_c_re[i], ssm_c_im[i], ssm_d[i])
            y = _s5_unpack(_s5(_s5_pack(u, n_chunks_pad), kmat, bmat, cmat, a_pow), lp)
            hres = _even_out(o, ga, y, gb, hres, w_glu[i].astype(BF16), row2(b_glu[i]),
                             w_out_even[i].astype(BF16), gpost)
        else:
            xm, z = _odd_in(hres, gpre, w_in_odd[i].astype(BF16))
            wqk = jnp.concatenate([w_q[i], w_k[i]], axis=2).astype(BF16)
            wg = jnp.concatenate([w_igate[i, 0], w_igate[i, 1], w_fgate[i, 0], w_fgate[i, 1]], axis=1)
            wg = jnp.pad(wg, ((0, 0), (0, GATE_LANES - 4 * ML_H))).astype(BF16)
            bg = jnp.concatenate([b_igate[i, 0], b_igate[i, 1], b_fgate[i, 0], b_fgate[i, 1]])
            bg = jnp.pad(bg, (0, GATE_LANES - 4 * ML_H)).reshape(1, GATE_LANES)
            qm, km, vm, xc, gates = _odd_mid(xm, conv_w[i], row2(conv_b[i]), wqk, w_v[i].astype(BF16), wg, bg)
            hf, hb = _mlstm(qm, km, vm, gates)
            hres = _odd_out(hf, hb, xc, z, hres, w_ogate[i].astype(BF16), row2(b_ogate[i]), row2(head_norm[i]),
                            row2(mlstm_skip[i]), w_out_odd[i].astype(BF16), gpost)
    return hres[FRONT:].reshape(1, n_real, D_MODEL)
```

```python
import functools
import math

import jax
import jax.numpy as jnp
from jax import lax
from jax.experimental import pallas as pl
from jax.experimental.pallas import tpu as pltpu

F32 = jnp.float32
BF16 = jnp.bfloat16

D_MODEL = 1024
N_META = 16
FRONT = 128
N_DUMMY = FRONT - N_META
GRID_W = 64
EPS = 1e-6
NEG = -1e30

HEADS = 8
KV_HEADS = 2
KV_REP = HEADS // KV_HEADS
HEAD_DIM = 128
ROPE_AXIS = HEAD_DIM // 2
ROPE_THETA = 10000.0
ATTN_W = HEADS * HEAD_DIM
KV_W = KV_HEADS * HEAD_DIM
TQ = 384
TQS = 128
TK = 1024
VT_ROWS = HEAD_DIM + 16
LOGIT_SAFE = 60.0

S5_W = 1024
S5_P = 16
S5_G = S5_W // S5_P
S5_N = 64
S5_T = 16
S5_GB = 8
S5_PB = S5_GB // 2
S5_CW = S5_T * S5_P

ML_W = 2048
ML_H = 8
ML_DV = 256
ML_DK = 128
CONV_W = 5
ML_CHUNK = 128
ML_AUG = 16
NEG_BIG = -1e9

TM = 384
VMEM_LIMIT = 56 * 1024 * 1024


def _cparams(*sem):
    return pltpu.CompilerParams(dimension_semantics=sem, vmem_limit_bytes=VMEM_LIMIT)


def _rms(x, g):
    return x * lax.rsqrt(jnp.mean(x * x, axis=-1, keepdims=True) + EPS) * g


def _sigmoid(x):
    return 1.0 / (1.0 + jnp.exp(-x))


def _silu(x):
    return x * _sigmoid(x)


def _gelu_tanh(x):
    return 0.5 * x * (1.0 + jnp.tanh(math.sqrt(2.0 / math.pi) * (x + 0.044715 * (x * x * x))))


def _dot(a, b):
    return jnp.dot(a, b, preferred_element_type=F32)


def _dot_nt(a, b):
    return lax.dot_general(a, b, (((1,), (1,)), ((), ())), preferred_element_type=F32)


def _dot_tn(a, b):
    return lax.dot_general(a, b, (((0,), (0,)), ((), ())), preferred_element_type=F32)


def _full(shape):
    return pl.BlockSpec(shape, lambda *_: (0,) * len(shape))


def _even_in_kernel(h_ref, gpre_ref, w_ref, wvt_ref, qg_ref, kg_ref, cos_ref, sin_ref,
                    q_ref, k_ref, vt_ref, ga_ref, u_ref, gb_ref):
    hn = _rms(h_ref[...], gpre_ref[...]).astype(BF16)
    cos = cos_ref[...]
    sin = sin_ref[...]
    lane = lax.broadcasted_iota(jnp.int32, cos.shape, 1)
    first_half = (lane % ROPE_AXIS) < (ROPE_AXIS // 2)

    def head(x, g):
        y = _rms(x, g)
        partner = jnp.where(first_half, pltpu.roll(y, HEAD_DIM - ROPE_AXIS // 2, 1),
                            pltpu.roll(y, ROPE_AXIS // 2, 1))
        return (y * cos + partner * sin).astype(BF16)

    qsec = _dot(hn, w_ref[:, 0:ATTN_W])
    for h in range(HEADS):
        sl = slice(h * HEAD_DIM, (h + 1) * HEAD_DIM)
        q_ref[:, sl] = head(qsec[:, sl], qg_ref[...])
    ksec = _dot(hn, w_ref[:, ATTN_W:ATTN_W + KV_W])
    for h in range(KV_HEADS):
        sl = slice(h * HEAD_DIM, (h + 1) * HEAD_DIM)
        k_ref[:, sl] = head(ksec[:, sl], kg_ref[...])
    vt = _dot_nt(wvt_ref[...], hn).astype(BF16)
    for g in range(KV_HEADS):
        vt_ref[g * VT_ROWS:g * VT_ROWS + HEAD_DIM, :] = vt[g * HEAD_DIM:(g + 1) * HEAD_DIM]
        vt_ref[g * VT_ROWS + HEAD_DIM:(g + 1) * VT_ROWS, :] = jnp.ones((VT_ROWS - HEAD_DIM, TM), BF16)
    c0 = ATTN_W + 2 * KV_W
    ga_ref[...] = _dot(hn, w_ref[:, c0:c0 + ATTN_W]).astype(BF16)
    c0 += ATTN_W
    u_ref[...] = _dot(hn, w_ref[:, c0:c0 + S5_W])
    c0 += S5_W
    gb_ref[...] = _dot(hn, w_ref[:, c0:c0 + S5_W]).astype(BF16)


def _even_in(hres, gpre, w_in, wvt, qg, kg, cos, sin):
    lp = hres.shape[0]
    win_w = w_in.shape[1]
    row = lambda w: pl.BlockSpec((TM, w), lambda i: (i, 0))
    vt_rows = KV_HEADS * VT_ROWS
    col = pl.BlockSpec((vt_rows, TM), lambda i: (0, i))
    widths = [ATTN_W, KV_W, None, ATTN_W, S5_W, S5_W]
    return pl.pallas_call(
        _even_in_kernel,
        grid=(lp // TM,),
        in_specs=[row(D_MODEL), _full((1, D_MODEL)), _full((D_MODEL, win_w)), _full((KV_W, D_MODEL)),
                  _full((1, HEAD_DIM)), _full((1, HEAD_DIM)), row(HEAD_DIM), row(HEAD_DIM)],
        out_specs=[col if w is None else row(w) for w in widths],
        out_shape=[jax.ShapeDtypeStruct((vt_rows, lp) if w is None else (lp, w), F32 if n == 4 else BF16)
                   for n, w in enumerate(widths)],
        compiler_params=_cparams("parallel"),
        name="even_in",
    )(hres, gpre, w_in, wvt, qg, kg, cos, sin)


def _attn_kernel(q_ref, k_ref, vt_ref, o_ref, m_sc, acc_sc, p_sc, *, n_kv_tiles, stabilised):
    qs = [jnp.concatenate([q_ref[a * TQS:(a + 1) * TQS, r * HEAD_DIM:(r + 1) * HEAD_DIM] for r in range(KV_REP)],
                          axis=0) for a in range(TQ // TQS)]

    def scores(kb, a, mask_dummy=False):
        st = _dot_nt(kb, qs[a])
        if mask_dummy:
            key = lax.broadcasted_iota(jnp.int32, st.shape, 0)
            st = jnp.where(key >= N_DUMMY, st, NEG)
        return st

    def tile_start(j):
        return pl.multiple_of(FRONT + j * TK, 128)

    k0, vt0 = k_ref[0:FRONT, :], vt_ref[:, 0:FRONT]
    if stabilised:
        def update(kb, vtb, mask_dummy):
            for a in range(len(qs)):
                st = scores(kb, a, mask_dummy)
                m_old = m_sc[a]
                m_new = jnp.maximum(m_old, jnp.max(st, axis=0, keepdims=True))
                p = jnp.exp2(st - m_new).astype(BF16)
                acc_sc[a] = jnp.exp2(m_old - m_new) * acc_sc[a] + _dot(vtb, p)
                m_sc[a] = m_new

        m_sc[...] = jnp.full(m_sc.shape, NEG, F32)
        acc_sc[...] = jnp.zeros(acc_sc.shape, F32)
        update(k0, vt0, True)

        def body(j, carry):
            update(k_ref[pl.ds(tile_start(j), TK), :], vt_ref[:, pl.ds(tile_start(j), TK)], False)
            return carry

        lax.fori_loop(0, n_kv_tiles, body, 0)
    else:
        for a in range(len(qs)):
            acc_sc[a] = _dot(vt0, jnp.exp2(scores(k0, a, True).astype(BF16)))
            p_sc[0, a] = jnp.exp2(scores(k_ref[FRONT:FRONT + TK, :], a).astype(BF16))

        def body(j, carry):
            cur, prev = j % 2, (j + 1) % 2
            kb = k_ref[pl.ds(tile_start(j), TK), :]
            vtb = vt_ref[:, pl.ds(tile_start(j - 1), TK)]
            sts = [scores(kb, a) for a in range(len(qs))]
            for a in range(len(qs)):
                acc_sc[a] += _dot(vtb, p_sc[prev, a])
            for a in range(len(qs)):
                p_sc[cur, a] = jnp.exp2(sts[a].astype(BF16))
            return carry

        lax.fori_loop(1, n_kv_tiles, body, 0)
        last = (n_kv_tiles - 1) % 2
        vtb = vt_ref[:, FRONT + (n_kv_tiles - 1) * TK:FRONT + n_kv_tiles * TK]
        for a in range(len(qs)):
            acc_sc[a] += _dot(vtb, p_sc[last, a])

    for a in range(TQ // TQS):
        ot = acc_sc[a, 0:HEAD_DIM, :] / acc_sc[a, HEAD_DIM:HEAD_DIM + 1, :]
        for r in range(KV_REP):
            o_ref[a * TQS:(a + 1) * TQS, r * HEAD_DIM:(r + 1) * HEAD_DIM] = (
                ot[:, r * TQS:(r + 1) * TQS].T.astype(BF16))


def _attention(q, k, vt, stabilised):
    lp = q.shape[0]
    n_kv_tiles = (lp - FRONT) // TK
    gw = KV_REP * HEAD_DIM
    n_sub = TQ // TQS
    return pl.pallas_call(
        functools.partial(_attn_kernel, n_kv_tiles=n_kv_tiles, stabilised=stabilised),
        grid=(KV_HEADS, lp // TQ),
        in_specs=[pl.BlockSpec((TQ, gw), lambda g, i: (i, g)),
                  pl.BlockSpec((lp, HEAD_DIM), lambda g, i: (0, g)),
                  pl.BlockSpec((VT_ROWS, lp), lambda g, i: (g, 0))],
        out_specs=pl.BlockSpec((TQ, gw), lambda g, i: (i, g)),
        out_shape=jax.ShapeDtypeStruct((lp, ATTN_W), BF16),
        scratch_shapes=[pltpu.VMEM((n_sub, 1, KV_REP * TQS), F32),
                        pltpu.VMEM((n_sub, VT_ROWS, KV_REP * TQS), F32),
                        pltpu.VMEM((2, n_sub, TK, KV_REP * TQS), BF16)],
        compiler_params=_cparams("parallel", "parallel"),
        name="attention_stabilised" if stabilised else "attention",
    )(q, k, vt)


def _s5_tables(a_re, a_im, log_dt, b_re, b_im, c_re, c_im, d_skip):
    t = S5_T
    lag = jnp.arange(t + 1, dtype=F32)
    per_dir = []
    for d in range(2):
        a = lax.complex(a_re[d], a_im[d])
        adt = a * jnp.exp(log_dt[d])[:, None]
        a_bar = jnp.exp(adt)
        b_bar = ((a_bar - 1.0) / a)[..., None] * lax.complex(b_re[d], b_im[d])
        c = lax.complex(c_re[d], c_im[d])
        pw = jnp.exp(adt[None] * lag[:, None, None].astype(jnp.complex64))
        klag = jnp.real(jnp.einsum('gon,dgn,gni->dgoi', c, pw[:t], b_bar,
                                   precision=lax.Precision.HIGHEST))
        per_dir.append((b_bar, c, pw, klag))
    (bb0, c0, pw0, kl0), (bb1, c1, pw1, kl1) = per_dir
    ii = jnp.arange(t)[:, None, None]
    jj = jnp.arange(t)[None, :, None]
    dd = jnp.arange(t)[None, None, :]
    place_f = (jj - ii == dd).astype(F32)
    place_b = (ii - jj == dd).astype(F32)
    skip = d_skip.reshape(S5_G, S5_P)[:, :, None] * jnp.eye(S5_P, dtype=F32)
    kl0 = kl0.at[0].add(skip)
    kmat = (jnp.einsum('ijd,dgop->gipjo', place_f, kl0, precision=lax.Precision.HIGHEST)
            + jnp.einsum('ijd,dgop->gipjo', place_b, kl1, precision=lax.Precision.HIGHEST)
            ).reshape(S5_G, S5_CW, S5_CW)

    bf = pw0[:t][::-1][..., None] * bb0[None]
    bb = pw1[:t][..., None] * bb1[None]
    cf = c0[None] * pw0[1:t + 1][:, :, None, :]
    cb = c1[None] * pw1[1:t + 1][::-1][:, :, None, :]

    def in_mat(x):
        return x.transpose(1, 0, 3, 2).reshape(S5_G, S5_CW, S5_N)

    def out_mat(x):
        return x.transpose(1, 3, 0, 2).reshape(S5_G, S5_N, S5_CW)

    n_pair = S5_G // 2
    zero_in = jnp.zeros((n_pair, S5_CW, S5_N), F32)
    zero_out = jnp.zeros((n_pair, S5_N, S5_CW), F32)

    def pair_in(x):
        e, o = x[0::2], x[1::2]
        return jnp.concatenate([jnp.concatenate([e, zero_in], axis=2),
                                jnp.concatenate([zero_in, o], axis=2)], axis=1)

    def pair_out(x):
        e, o = x[0::2], x[1::2]
        return jnp.concatenate([jnp.concatenate([e, zero_out], axis=2),
                                jnp.concatenate([zero_out, o], axis=2)], axis=1)

    bmat = jnp.concatenate([pair_in(in_mat(jnp.real(bf))), pair_in(in_mat(jnp.imag(bf))),
                            pair_in(in_mat(jnp.real(bb))), pair_in(in_mat(jnp.imag(bb)))], axis=2)
    cmat = jnp.concatenate([pair_out(out_mat(jnp.real(cf))), pair_out(out_mat(-jnp.imag(cf))),
                            pair_out(out_mat(jnp.real(cb))), pair_out(out_mat(-jnp.imag(cb)))], axis=1)

    def pair_row(x):
        return x.reshape(S5_G // S5_GB, 1, S5_PB * 2 * S5_N)

    a_pow = jnp.concatenate([pair_row(jnp.real(pw0[t])), pair_row(jnp.imag(pw0[t])),
                             pair_row(jnp.real(pw1[t])), pair_row(jnp.imag(pw1[t]))], axis=1)
    return kmat.astype(BF16), bmat.astype(BF16), cmat.astype(BF16), a_pow


LANE_GROUPS = 128 // S5_P


def _swap_token_group(xs):
    block = lax.broadcasted_iota(jnp.int32, xs[0].shape, 1) // S5_P
    xs = list(xs)
    d = LANE_GROUPS // 2
    while d:
        upper = (block & d) != 0
        for a in range(LANE_GROUPS):
            if a & d:
                continue
            lo, hi = xs[a], xs[a + d]
            xs[a] = jnp.where(upper, pltpu.roll(hi, S5_P * d, 1), lo)
            xs[a + d] = jnp.where(upper, hi, pltpu.roll(lo, 128 - S5_P * d, 1))
        d //= 2
    return xs


def _s5_pack_kernel(u_ref, o_ref, *, n_chunks):
    n_main = n_chunks // 16 * 16
    for half in range(S5_T // LANE_GROUPS):
        xs = [u_ref[pl.ds(LANE_GROUPS * half + i, n_chunks, stride=S5_T), :] for i in range(LANE_GROUPS)]
        for g, y in enumerate(_swap_token_group(xs)):
            cols = slice(g * S5_CW + half * 128, g * S5_CW + (half + 1) * 128)
            o_ref[0:n_main, cols] = y[0:n_main].astype(BF16)
            if o_ref.shape[0] > n_main:
                tail = jnp.concatenate([y[n_main:], jnp.zeros((o_ref.shape[0] - n_chunks, 128), F32)], axis=0)
                o_ref[n_main:, cols] = tail.astype(BF16)


def _s5_unpack_kernel(y_ref, o_ref, *, n_chunks):
    for half in range(S5_T // LANE_GROUPS):
        ys = [y_ref[:, g * S5_CW + half * 128:g * S5_CW + (half + 1) * 128].astype(F32)[0:n_chunks]
              for g in range(LANE_GROUPS)]
        for i, x in enumerate(_swap_token_group(ys)):
            o_ref[pl.ds(LANE_GROUPS * half + i, n_chunks, stride=S5_T), :] = x


def _s5_pack(u, n_chunks_pad):
    lp = u.shape[0]
    bw = LANE_GROUPS * S5_CW
    return pl.pallas_call(
        functools.partial(_s5_pack_kernel, n_chunks=lp // S5_T),
        grid=(S5_W // 128,),
        in_specs=[pl.BlockSpec((lp, 128), lambda i: (0, i))],
        out_specs=pl.BlockSpec((n_chunks_pad, bw), lambda i: (0, i)),
        out_shape=jax.ShapeDtypeStruct((n_chunks_pad, S5_G * S5_CW), BF16),
        compiler_params=_cparams("parallel"),
        name="s5_pack",
    )(u)


def _s5_unpack(yc, lp):
    bw = LANE_GROUPS * S5_CW
    return pl.pallas_call(
        functools.partial(_s5_unpack_kernel, n_chunks=lp // S5_T),
        grid=(S5_W // 128,),
        in_specs=[pl.BlockSpec((yc.shape[0], bw), lambda i: (0, i))],
        out_specs=pl.BlockSpec((lp, 128), lambda i: (0, i)),
        out_shape=jax.ShapeDtypeStruct((lp, S5_W), F32),
        compiler_params=_cparams("parallel"),
        name="s5_unpack",
    )(yc)


def _s5_kernel(u_ref, k_ref, b_ref, c_ref, a_ref, y_ref, fre, fim, bre, bim, *, n_chunks):
    pw = 2 * S5_N
    for p in range(S5_PB):
        s = _dot(u_ref[:, p * 2 * S5_CW:(p + 1) * 2 * S5_CW], b_ref[p])
        cols = slice(p * pw, (p + 1) * pw)
        fre[:, cols] = s[:, 0:pw]
        fim[:, cols] = s[:, pw:2 * pw]
        bre[:, cols] = s[:, 2 * pw:3 * pw]
        bim[:, cols] = s[:, 3 * pw:4 * pw]

    far, fai, bar, bai = a_ref[0, 0:1, :], a_ref[0, 1:2, :], a_ref[0, 2:3, :], a_ref[0, 3:4, :]

    def step(i, carry):
        xr, xi, yr, yi = carry
        rf = pl.ds(i, 1)
        rb = pl.ds(n_chunks - 1 - i, 1)
        sr, si = fre[rf, :], fim[rf, :]
        tr, ti = bre[rb, :], bim[rb, :]
        fre[rf, :] = xr
        fim[rf, :] = xi
        bre[rb, :] = yr
        bim[rb, :] = yi
        return (far * xr - fai * xi + sr, far * xi + fai * xr + si,
                bar * yr - bai * yi + tr, bar * yi + bai * yr + ti)

    zero = jnp.zeros((1, S5_PB * pw), F32)
    lax.fori_loop(0, n_chunks, step, (zero, zero, zero, zero))

    for p in range(S5_PB):
        cols = slice(p * pw, (p + 1) * pw)
        state = jnp.concatenate([fre[:, cols], fim[:, cols], bre[:, cols], bim[:, cols]], axis=1).astype(BF16)
        carry_out = _dot(state, c_ref[p])
        for e in range(2):
            g = 2 * p + e
            gc = slice(g * S5_CW, (g + 1) * S5_CW)
            y = _dot(u_ref[:, gc], k_ref[g]) + carry_out[:, e * S5_CW:(e + 1) * S5_CW]
            y_ref[:, gc] = y.astype(BF16)


def _s5(u_chunks, kmat, bmat, cmat, a_pow):
    n_chunks = u_chunks.shape[0]
    bw = S5_GB * S5_CW
    sw = S5_PB * 2 * S5_N
    return pl.pallas_call(
        functools.partial(_s5_kernel, n_chunks=n_chunks),
        grid=(S5_G // S5_GB,),
        in_specs=[pl.BlockSpec((n_chunks, bw), lambda i: (0, i)),
                  pl.BlockSpec((S5_GB, S5_CW, S5_CW), lambda i: (i, 0, 0)),
                  pl.BlockSpec((S5_PB, 2 * S5_CW, 8 * S5_N), lambda i: (i, 0, 0)),
                  pl.BlockSpec((S5_PB, 8 * S5_N, 2 * S5_CW), lambda i: (i, 0, 0)),
                  pl.BlockSpec((1, 4, sw), lambda i: (i, 0, 0))],
        out_specs=pl.BlockSpec((n_chunks, bw), lambda i: (0, i)),
        out_shape=jax.ShapeDtypeStruct(u_chunks.shape, BF16),
        scratch_shapes=[pltpu.VMEM((n_chunks, sw), F32)] * 4,
        compiler_params=_cparams("parallel"),
        name="s5_scan",
    )(u_chunks, kmat, bmat, cmat, a_pow)


def _row_keep(shape, tile_rows):
    row = pl.program_id(0) * tile_rows + lax.broadcasted_iota(jnp.int32, shape, 0)
    return row >= N_DUMMY


def _even_out_kernel(o_ref, ga_ref, y_ref, gb_ref, h_ref, wglu_ref, bglu_ref, wout_ref, gpost_ref, hout_ref):
    ya = o_ref[...].astype(F32) * _silu(ga_ref[...].astype(F32))
    yb = _gelu_tanh(y_ref[...].astype(F32))
    yb = yb * _sigmoid(_dot(yb.astype(BF16), wglu_ref[...]) + bglu_ref[...])
    yb = yb * _silu(gb_ref[...].astype(F32))
    out = _dot(ya.astype(BF16), wout_ref[0:ATTN_W, :]) + _dot(yb.astype(BF16), wout_ref[ATTN_W:, :])
    hnew = h_ref[...] + _rms(out, gpost_ref[...])
    hout_ref[...] = jnp.where(_row_keep(hnew.shape, TM), hnew, 0.0)


def _even_out(o, ga, y, gb, hres, wglu, bglu, wout, gpost):
    lp = hres.shape[0]
    row = lambda w: pl.BlockSpec((TM, w), lambda i: (i, 0))
    return pl.pallas_call(
        _even_out_kernel,
        grid=(lp // TM,),
        in_specs=[row(ATTN_W), row(ATTN_W), row(S5_W), row(S5_W), row(D_MODEL),
                  _full((S5_W, S5_W)), _full((1, S5_W)), _full((ATTN_W + S5_W, D_MODEL)), _full((1, D_MODEL))],
        out_specs=row(D_MODEL),
        out_shape=jax.ShapeDtypeStruct((lp, D_MODEL), F32),
        compiler_params=_cparams("parallel"),
        name="even_out",
    )(o, ga, y, gb, hres, wglu, bglu, wout, gpost)


def _odd_in_kernel(h_ref, gpre_ref, w_ref, xm_ref, z_ref):
    hn = _rms(h_ref[...], gpre_ref[...]).astype(BF16)
    xm_ref[...] = _dot(hn, w_ref[:, 0:ML_W]).astype(BF16)
    z_ref[...] = _dot(hn, w_ref[:, ML_W:]).astype(BF16)


def _odd_in(hres, gpre, w_in):
    lp = hres.shape[0]
    row = lambda w: pl.BlockSpec((TM, w), lambda i: (i, 0))
    return pl.pallas_call(
        _odd_in_kernel,
        grid=(lp // TM,),
        in_specs=[row(D_MODEL), _full((1, D_MODEL)), _full((D_MODEL, 2 * ML_W))],
        out_specs=[row(ML_W), row(ML_W)],
        out_shape=[jax.ShapeDtypeStruct((lp, ML_W), BF16)] * 2,
        compiler_params=_cparams("parallel"),
        name="odd_in",
    )(hres, gpre, w_in)


HALO = 8
GATE_LANES = 128


def _odd_mid_kernel(xm_ref, prev_ref, next_ref, cw_ref, cb_ref, wqk_ref, wv_ref, wg_ref, bg_ref,
                    q_ref, k_ref, v_ref, xc_ref, g_ref, ext):
    i = pl.program_id(0)
    n = pl.num_programs(0)
    xm = xm_ref[...]
    ext[0:HALO, :] = jnp.where(i > 0, prev_ref[...].astype(F32), 0.0)
    ext[HALO:HALO + TM, :] = xm.astype(F32)
    ext[HALO + TM:, :] = jnp.where(i < n - 1, next_ref[...].astype(F32), 0.0)
    conv = cb_ref[...]
    for j in range(CONV_W):
        conv = conv + cw_ref[j:j + 1, :] * ext[pl.ds(HALO - CONV_W // 2 + j, TM), :]
    xc = _silu(conv)
    xcb = xc.astype(BF16)
    xc_ref[...] = xcb

    for h in range(ML_H):
        cin = slice(h * ML_DV, (h + 1) * ML_DV)
        qk = _dot(xcb[:, cin], wqk_ref[h])
        q_ref[:, h * ML_DK:(h + 1) * ML_DK] = (qk[:, 0:ML_DK] * (ML_DK ** -0.5)).astype(BF16)
        k_ref[:, h * ML_DK:(h + 1) * ML_DK] = qk[:, ML_DK:].astype(BF16)
        v_ref[:, cin] = _dot(xm[:, cin], wv_ref[h]).astype(BF16)

    pre = _dot(xcb, wg_ref[...]) + bg_ref[...]
    lane = lax.broadcasted_iota(jnp.int32, pre.shape, 1)
    keep = _row_keep(pre.shape, TM)
    li = jnp.where(keep, pre, NEG_BIG)
    lf = jnp.where(keep, jnp.minimum(pre, 0.0) - jnp.log(1.0 + jnp.exp(-jnp.abs(pre))), 0.0)
    t = lax.broadcasted_iota(jnp.int32, (TM, TM), 0)
    s = lax.broadcasted_iota(jnp.int32, (TM, TM), 1)
    same = (t // ML_CHUNK) == (s // ML_CHUNK)
    tri_f = jnp.where(same & (s <= t), 1.0, 0.0).astype(BF16)
    tri_b = jnp.where(same & (s >= t), 1.0, 0.0).astype(BF16)
    lf_hi = lf.astype(BF16)
    lf_lo = (lf - lf_hi.astype(F32)).astype(BF16)
    cum_f = _dot(tri_f, lf_hi) + _dot(tri_f, lf_lo)
    cum_b = _dot(tri_b, lf_hi) + _dot(tri_b, lf_lo)
    g_ref[...] = jnp.where(lane < 2 * ML_H, li, jnp.where(lane < 3 * ML_H, cum_f, cum_b))


def _odd_mid(xm, cw, cb, wqk, wv, wg, bg):
    lp = xm.shape[0]
    row = lambda w: pl.BlockSpec((TM, w), lambda i: (i, 0))
    per = TM // HALO
    last = lp // HALO - 1
    return pl.pallas_call(
        _odd_mid_kernel,
        grid=(lp // TM,),
        in_specs=[row(ML_W),
                  pl.BlockSpec((HALO, ML_W), lambda i: (jnp.maximum(i * per - 1, 0), 0)),
                  pl.BlockSpec((HALO, ML_W), lambda i: (jnp.minimum((i + 1) * per, last), 0)),
                  _full((CONV_W, ML_W)), _full((1, ML_W)),
                  _full((ML_H, ML_DV, 2 * ML_DK)), _full((ML_H, ML_DV, ML_DV)),
                  _full((ML_W, GATE_LANES)), _full((1, GATE_LANES))],
        out_specs=[row(ML_H * ML_DK), row(ML_H * ML_DK), row(ML_W), row(ML_W), row(GATE_LANES)],
        out_shape=[jax.ShapeDtypeStruct((lp, ML_H * ML_DK), BF16), jax.ShapeDtypeStruct((lp, ML_H * ML_DK), BF16),
                   jax.ShapeDtypeStruct((lp, ML_W), BF16), jax.ShapeDtypeStruct((lp, ML_W), BF16),
                   jax.ShapeDtypeStruct((lp, GATE_LANES), F32)],
        scratch_shapes=[pltpu.VMEM((TM + 2 * HALO, ML_W), F32)],
        compiler_params=_cparams("parallel"),
        name="odd_mid",
    )(xm, xm, xm, cw, cb, wqk, wv, wg, bg)


def _mlstm_chunk(q, k, v, g, gt, h, backward, c_sc, m_sc, slot):
    c = ML_CHUNK
    off = ML_H if backward else 0
    li_row = gt[off + h:off + h + 1, :]
    b_row = gt[2 * ML_H + off + h:2 * ML_H + off + h + 1, :]
    a_row = li_row - b_row
    a_col = g[:, off + h:off + h + 1] - g[:, 2 * ML_H + off + h:2 * ML_H + off + h + 1]
    b_last = jnp.broadcast_to(b_row[:, 0:1] if backward else b_row[:, c - 1:c], (1, c))
    m_prev = m_sc[slot]
    state = c_sc[slot]

    kq = _dot_nt(k, q)
    vt_aug = jnp.concatenate([v.T, jnp.ones((ML_AUG, c), BF16)], axis=0)
    qt = q.T.astype(F32)
    m_new = b_last + jnp.maximum(m_prev, jnp.max(a_row, axis=-1, keepdims=True))
    decay = jnp.exp(b_last + m_prev - m_new)
    w_row = jnp.exp(b_last + a_row - m_new)
    update = _dot((vt_aug.astype(F32) * w_row).astype(BF16), k)
    yield None

    s = lax.broadcasted_iota(jnp.int32, (c, c), 0)
    t = lax.broadcasted_iota(jnp.int32, (c, c), 1)
    allowed = (s >= t) if backward else (s <= t)
    dt = jnp.where(allowed, b_row + a_col, NEG)
    inter_log = b_row + m_prev
    m_t = jnp.maximum(jnp.max(dt, axis=0, keepdims=True), inter_log)
    swt = jnp.exp(dt - m_t) * kq
    inter = jnp.exp(inter_log - m_t)
    num = _dot(jnp.concatenate([vt_aug, state.astype(BF16)], axis=1),
               jnp.concatenate([swt.astype(BF16), (qt * inter).astype(BF16)], axis=0))
    yield None

    den = num[ML_DV:ML_DV + 1, :]
    out_t = num[0:ML_DV, :] / jnp.maximum(jnp.abs(den), jnp.exp(-m_t))
    c_sc[slot] = decay * state + update
    m_sc[slot] = m_new
    yield out_t.T


def _mlstm_kernel(qf_ref, kf_ref, vf_ref, gf_ref, qb_ref, kb_ref, vb_ref, gb_ref, hf_ref, hb_ref, c_sc, m_sc):
    @pl.when(pl.program_id(0) == 0)
    def _():
        c_sc[...] = jnp.zeros(c_sc.shape, F32)
        m_sc[...] = jnp.zeros(m_sc.shape, F32)

    chunks = []
    for backward, (q_ref, k_ref, v_ref, g_ref, h_ref) in enumerate(
            ((qf_ref, kf_ref, vf_ref, gf_ref, hf_ref), (qb_ref, kb_ref, vb_ref, gb_ref, hb_ref))):
        g = g_ref[...]
        gt = g.T
        for h in range(ML_H):
            gen = _mlstm_chunk(q_ref[:, h * ML_DK:(h + 1) * ML_DK], k_ref[:, h * ML_DK:(h + 1) * ML_DK],
                               v_ref[:, h * ML_DV:(h + 1) * ML_DV], g, gt, h, bool(backward),
                               c_sc, m_sc, backward * ML_H + h)
            chunks.append((gen, h_ref, h))
    for gen, _, _ in chunks:
        next(gen)
    for gen, _, _ in chunks:
        next(gen)
    for gen, h_ref, h in chunks:
        h_ref[:, h * ML_DV:(h + 1) * ML_DV] = next(gen).astype(BF16)


def _mlstm(q, k, v, gates):
    lp = q.shape[0]
    nch = lp // ML_CHUNK
    fwd = lambda w: pl.BlockSpec((ML_CHUNK, w), lambda i: (i, 0))
    bwd = lambda w: pl.BlockSpec((ML_CHUNK, w), lambda i: (nch - 1 - i, 0))
    qk_w = ML_H * ML_DK
    return pl.pallas_call(
        _mlstm_kernel,
        grid=(nch,),
        in_specs=[fwd(qk_w), fwd(qk_w), fwd(ML_W), fwd(GATE_LANES),
                  bwd(qk_w), bwd(qk_w), bwd(ML_W), bwd(GATE_LANES)],
        out_specs=[fwd(ML_W), bwd(ML_W)],
        out_shape=[jax.ShapeDtypeStruct((lp, ML_W), BF16)] * 2,
        scratch_shapes=[pltpu.VMEM((2 * ML_H, ML_DV + ML_AUG, ML_DK), F32),
                        pltpu.VMEM((2 * ML_H, 1, ML_CHUNK), F32)],
        compiler_params=_cparams("arbitrary"),
        name="mlstm",
    )(q, k, v, gates, q, k, v, gates)


def _odd_out_kernel(hf_ref, hb_ref, xc_ref, z_ref, h_ref, wo_ref, bo_ref, hn_ref, sk_ref, wout_ref, gpost_ref,
                    hout_ref):
    xcb = xc_ref[...]
    xc = xcb.astype(F32)
    og = _sigmoid(_dot(xcb, wo_ref[...]) + bo_ref[...])
    cell = hf_ref[...].astype(F32) + hb_ref[...].astype(F32)
    parts = []
    for h in range(ML_H):
        sl = slice(h * ML_DV, (h + 1) * ML_DV)
        parts.append(_rms(cell[:, sl], hn_ref[:, sl]))
    cell = jnp.concatenate(parts, axis=1)
    hh = (og * cell + sk_ref[...] * xc) * _silu(z_ref[...].astype(F32))
    out = _dot(hh.astype(BF16), wout_ref[...])
    hnew = h_ref[...] + _rms(out, gpost_ref[...])
    hout_ref[...] = jnp.where(_row_keep(hnew.shape, TM), hnew, 0.0)


def _odd_out(hf, hb, xc, z, hres, wo, bo, hn, sk, wout, gpost):
    lp = hres.shape[0]
    row = lambda w: pl.BlockSpec((TM, w), lambda i: (i, 0))
    return pl.pallas_call(
        _odd_out_kernel,
        grid=(lp // TM,),
        in_specs=[row(ML_W), row(ML_W), row(ML_W), row(ML_W), row(D_MODEL),
                  _full((ML_W, ML_W)), _full((1, ML_W)), _full((1, ML_W)), _full((1, ML_W)),
                  _full((ML_W, D_MODEL)), _full((1, D_MODEL))],
        out_specs=row(D_MODEL),
        out_shape=jax.ShapeDtypeStruct((lp, D_MODEL), F32),
        compiler_params=_cparams("parallel"),
        name="odd_out",
    )(hf, hb, xc, z, hres, wo, bo, hn, sk, wout, gpost)


def _rope_tables(n_real):
    n = jnp.arange(n_real)
    zeros = jnp.zeros((FRONT,), F32)
    row = jnp.concatenate([zeros, (n // GRID_W).astype(F32)])
    col = jnp.concatenate([zeros, (n % GRID_W).astype(F32)])
    freqs = ROPE_THETA ** (-jnp.arange(0, ROPE_AXIS, 2, dtype=F32) / ROPE_AXIS)
    ang_r = row[:, None] * freqs[None]
    ang_c = col[:, None] * freqs[None]
    cos = jnp.concatenate([jnp.cos(ang_r)] * 2 + [jnp.cos(ang_c)] * 2, axis=1)
    sin = jnp.concatenate([-jnp.sin(ang_r), jnp.sin(ang_r), -jnp.sin(ang_c), jnp.sin(ang_c)], axis=1)
    return cos, sin


def kernel(x, meta_tokens, norm_pre, norm_post, w_in_even, q_norm, k_norm, ssm_a_re, ssm_a_im, ssm_log_dt, ssm_b_re, ssm_b_im, ssm_c_re, ssm_c_im, ssm_d, w_glu, b_glu, w_out_even, w_in_odd, conv_w, conv_b, w_q, w_k, w_v, w_igate, b_igate, w_fgate, b_fgate, w_ogate, b_ogate, head_norm, mlstm_skip, w_out_odd):
    bsz, n_real, _ = x.shape
    assert bsz == 1 and n_real % TK == 0 and (n_real + FRONT) % TM == 0
    lp = n_real + FRONT
    depth = norm_pre.shape[0]
    hres = jnp.concatenate([jnp.zeros((N_DUMMY, D_MODEL), F32), meta_tokens.astype(F32), x[0]], axis=0)
    cos, sin = _rope_tables(n_real)
    n_chunks = lp // S5_T
    n_chunks_pad = -(-n_chunks // 16) * 16
    row2 = lambda a: a.reshape(1, -1)

    for layer in range(depth):
        i = layer // 2
        gpre, gpost = row2(norm_pre[layer]), row2(norm_post[layer])
        if layer % 2 == 0:
            w_in = w_in_even[i].astype(BF16)
            wvt = w_in[:, ATTN_W + KV_W:ATTN_W + 2 * KV_W].T
            qg = row2(q_norm[i]) * (HEAD_DIM ** -0.5 * math.log2(math.e))
            q, k, vt, ga, u, gb = _even_in(hres, gpre, w_in, wvt, qg, row2(k_norm[i]), cos, sin)
            logit_bound = (math.sqrt(HEAD_DIM) * math.log2(math.e)
                           * jnp.max(jnp.abs(q_norm[i])) * jnp.max(jnp.abs(k_norm[i])))
            o = lax.cond(logit_bound <= LOGIT_SAFE,
                         lambda q, k, vt: _attention(q, k, vt, False),
                         lambda q, k, vt: _attention(q, k, vt, True), q, k, vt)
            kmat, bmat, cmat, a_pow = _s5_tables(ssm_a_re[i], ssm_a_im[i], ssm_log_dt[i], ssm_b_re[i], ssm_b_im[i],
                                                 ssm_c_re[i], ssm_c_im[i], ssm_d[i])
            y = _s5_unpack(_s5(_s5_pack(u, n_chunks_pad), kmat, bmat, cmat, a_pow), lp)
            hres = _even_out(o, ga, y, gb, hres, w_glu[i].astype(BF16), row2(b_glu[i]),
                             w_out_even[i].astype(BF16), gpost)
        else:
            xm, z = _odd_in(hres, gpre, w_in_odd[i].astype(BF16))
            wqk = jnp.concatenate([w_q[i], w_k[i]], axis=2).astype(BF16)
            wg = jnp.concatenate([w_igate[i, 0], w_igate[i, 1], w_fgate[i, 0], w_fgate[i, 1]], axis=1)
            wg = jnp.pad(wg, ((0, 0), (0, GATE_LANES - 4 * ML_H))).astype(BF16)
            bg = jnp.concatenate([b_igate[i, 0], b_igate[i, 1], b_fgate[i, 0], b_fgate[i, 1]])
            bg = jnp.pad(bg, (0, GATE_LANES - 4 * ML_H)).reshape(1, GATE_LANES)
            qm, km, vm, xc, gates = _odd_mid(xm, conv_w[i], row2(conv_b[i]), wqk, w_v[i].astype(BF16), wg, bg)
            hf, hb = _mlstm(qm, km, vm, gates)
            hres = _odd_out(hf, hb, xc, z, hres, w_ogate[i].astype(BF16), row2(b_ogate[i]), row2(head_norm[i]),
                            row2(mlstm_skip[i]), w_out_odd[i].astype(BF16), gpost)
    return hres[FRONT:].reshape(1, n_real, D_MODEL)
```

```python
import functools
import math

import jax
import jax.numpy as jnp
from jax import lax
from jax.experimental import pallas as pl
from jax.experimental.pallas import tpu as pltpu

F32 = jnp.float32
BF16 = jnp.bfloat16

D_MODEL = 1024
N_META = 16
FRONT = 128
N_DUMMY = FRONT - N_META
GRID_W = 64
EPS = 1e-6
NEG = -1e30

HEADS = 8
KV_HEADS = 2
KV_REP = HEADS // KV_HEADS
HEAD_DIM = 128
ROPE_AXIS = HEAD_DIM // 2
ROPE_THETA = 10000.0
ATTN_W = HEADS * HEAD_DIM
KV_W = KV_HEADS * HEAD_DIM
TQ = 384
TQS = 128
TK = 2048
VT_ROWS = HEAD_DIM + 16
LOGIT_SAFE = 60.0

S5_W = 1024
S5_P = 16
S5_G = S5_W // S5_P
S5_N = 64
S5_T = 16
S5_GB = 8
S5_PB = S5_GB // 2
S5_CW = S5_T * S5_P

ML_W = 2048
ML_H = 8
ML_DV = 256
ML_DK = 128
CONV_W = 5
ML_CHUNK = 128
ML_AUG = 16
NEG_BIG = -1e9

TM = 384
VMEM_LIMIT = 56 * 1024 * 1024


def _cparams(*sem):
    return pltpu.CompilerParams(dimension_semantics=sem, vmem_limit_bytes=VMEM_LIMIT)


def _rms(x, g):
    return x * lax.rsqrt(jnp.mean(x * x, axis=-1, keepdims=True) + EPS) * g


def _sigmoid(x):
    return 1.0 / (1.0 + jnp.exp(-x))


def _silu(x):
    return x * _sigmoid(x)


def _gelu_tanh(x):
    return 0.5 * x * (1.0 + jnp.tanh(math.sqrt(2.0 / math.pi) * (x + 0.044715 * (x * x * x))))


def _dot(a, b):
    return jnp.dot(a, b, preferred_element_type=F32)


def _dot_nt(a, b):
    return lax.dot_general(a, b, (((1,), (1,)), ((), ())), preferred_element_type=F32)


def _dot_tn(a, b):
    return lax.dot_general(a, b, (((0,), (0,)), ((), ())), preferred_element_type=F32)


def _slab(layer):
    def spec(shape):
        return pl.BlockSpec((None,) + tuple(shape), lambda *_: (layer,) + (0,) * len(shape))
    return spec


def _even_in_kernel(h_ref, gpre_ref, w_ref, wvt_ref, qg_ref, kg_ref, cos_ref, sin_ref,
                    q_ref, k_ref, vt_ref, ga_ref, u_ref, gb_ref):
    hn = _rms(h_ref[...], gpre_ref[...]).astype(BF16)
    cos = cos_ref[...]
    sin = sin_ref[...]
    lane = lax.broadcasted_iota(jnp.int32, cos.shape, 1)
    first_half = (lane % ROPE_AXIS) < (ROPE_AXIS // 2)

    def head(x, g):
        y = _rms(x, g)
        partner = jnp.where(first_half, pltpu.roll(y, HEAD_DIM - ROPE_AXIS // 2, 1),
                            pltpu.roll(y, ROPE_AXIS // 2, 1))
        return (y * cos + partner * sin).astype(BF16)

    qsec = _dot(hn, w_ref[:, 0:ATTN_W])
    for h in range(HEADS):
        sl = slice(h * HEAD_DIM, (h + 1) * HEAD_DIM)
        q_ref[:, sl] = head(qsec[:, sl], qg_ref[...])
    ksec = _dot(hn, w_ref[:, ATTN_W:ATTN_W + KV_W])
    for h in range(KV_HEADS):
        sl = slice(h * HEAD_DIM, (h + 1) * HEAD_DIM)
        k_ref[:, sl] = head(ksec[:, sl], kg_ref[...])
    vt = _dot_nt(wvt_ref[...], hn).astype(BF16)
    for g in range(KV_HEADS):
        vt_ref[g * VT_ROWS:g * VT_ROWS + HEAD_DIM, :] = vt[g * HEAD_DIM:(g + 1) * HEAD_DIM]
        vt_ref[g * VT_ROWS + HEAD_DIM:(g + 1) * VT_ROWS, :] = jnp.ones((VT_ROWS - HEAD_DIM, TM), BF16)
    c0 = ATTN_W + 2 * KV_W
    ga_ref[...] = _dot(hn, w_ref[:, c0:c0 + ATTN_W]).astype(BF16)
    c0 += ATTN_W
    u_ref[...] = _dot(hn, w_ref[:, c0:c0 + S5_W])
    c0 += S5_W
    gb_ref[...] = _dot(hn, w_ref[:, c0:c0 + S5_W]).astype(BF16)


def _even_in(layer, i, hres, gpre, w_in, wvt, qg, kg, cos, sin):
    lp = hres.shape[0]
    win_w = w_in.shape[-1]
    _full, _full_l = _slab(i), _slab(layer)
    row = lambda w: pl.BlockSpec((TM, w), lambda i: (i, 0))
    vt_rows = KV_HEADS * VT_ROWS
    col = pl.BlockSpec((vt_rows, TM), lambda i: (0, i))
    widths = [ATTN_W, KV_W, None, ATTN_W, S5_W, S5_W]
    return pl.pallas_call(
        _even_in_kernel,
        grid=(lp // TM,),
        in_specs=[row(D_MODEL), _full_l((1, D_MODEL)), _full((D_MODEL, win_w)), _full((KV_W, D_MODEL)),
                  _full((1, HEAD_DIM)), _full((1, HEAD_DIM)), row(HEAD_DIM), row(HEAD_DIM)],
        out_specs=[col if w is None else row(w) for w in widths],
        out_shape=[jax.ShapeDtypeStruct((vt_rows, lp) if w is None else (lp, w), F32 if n == 4 else BF16)
                   for n, w in enumerate(widths)],
        compiler_params=_cparams("parallel"),
        name="even_in",
    )(hres, gpre, w_in, wvt, qg, kg, cos, sin)


def _attn_kernel(q_ref, k_ref, vt_ref, o_ref, m_sc, acc_sc, p_sc, *, n_kv_tiles, stabilised):
    qs = [jnp.concatenate([q_ref[a * TQS:(a + 1) * TQS, r * HEAD_DIM:(r + 1) * HEAD_DIM] for r in range(KV_REP)],
                          axis=0) for a in range(TQ // TQS)]

    def scores(kb, a, mask_dummy=False):
        st = _dot_nt(kb, qs[a])
        if mask_dummy:
            key = lax.broadcasted_iota(jnp.int32, st.shape, 0)
            st = jnp.where(key >= N_DUMMY, st, NEG)
        return st

    def tile_start(j):
        return pl.multiple_of(FRONT + j * TK, 128)

    k0, vt0 = k_ref[0:FRONT, :], vt_ref[:, 0:FRONT]
    if stabilised:
        def update(kb, vtb, mask_dummy):
            for a in range(len(qs)):
                st = scores(kb, a, mask_dummy)
                m_old = m_sc[a]
                m_new = jnp.maximum(m_old, jnp.max(st, axis=0, keepdims=True))
                p = jnp.exp2(st - m_new).astype(BF16)
                acc_sc[a] = jnp.exp2(m_old - m_new) * acc_sc[a] + _dot(vtb, p)
                m_sc[a] = m_new

        m_sc[...] = jnp.full(m_sc.shape, NEG, F32)
        acc_sc[...] = jnp.zeros(acc_sc.shape, F32)
        update(k0, vt0, True)

        def body(j, carry):
            update(k_ref[pl.ds(tile_start(j), TK), :], vt_ref[:, pl.ds(tile_start(j), TK)], False)
            return carry

        lax.fori_loop(0, n_kv_tiles, body, 0)
    else:
        for a in range(len(qs)):
            acc_sc[a] = _dot(vt0, jnp.exp2(scores(k0, a, True).astype(BF16)))
            p_sc[0, a] = jnp.exp2(scores(k_ref[FRONT:FRONT + TK, :], a).astype(BF16))

        def body(j, carry):
            cur, prev = j % 2, (j + 1) % 2
            kb = k_ref[pl.ds(tile_start(j), TK), :]
            vtb = vt_ref[:, pl.ds(tile_start(j - 1), TK)]
            for a in range(len(qs)):
                st = scores(kb, a)
                acc_sc[a] += _dot(vtb, p_sc[prev, a])
                p_sc[cur, a] = jnp.exp2(st.astype(BF16))
            return carry

        lax.fori_loop(1, n_kv_tiles, body, 0)
        last = (n_kv_tiles - 1) % 2
        vtb = vt_ref[:, FRONT + (n_kv_tiles - 1) * TK:FRONT + n_kv_tiles * TK]
        for a in range(len(qs)):
            acc_sc[a] += _dot(vtb, p_sc[last, a])

    for a in range(TQ // TQS):
        ot = acc_sc[a, 0:HEAD_DIM, :] / acc_sc[a, HEAD_DIM:HEAD_DIM + 1, :]
        for r in range(KV_REP):
            o_ref[a * TQS:(a + 1) * TQS, r * HEAD_DIM:(r + 1) * HEAD_DIM] = (
                ot[:, r * TQS:(r + 1) * TQS].T.astype(BF16))


def _attention(q, k, vt, stabilised):
    lp = q.shape[0]
    n_kv_tiles = (lp - FRONT) // TK
    gw = KV_REP * HEAD_DIM
    n_sub = TQ // TQS
    return pl.pallas_call(
        functools.partial(_attn_kernel, n_kv_tiles=n_kv_tiles, stabilised=stabilised),
        grid=(KV_HEADS, lp // TQ),
        in_specs=[pl.BlockSpec((TQ, gw), lambda g, i: (i, g)),
                  pl.BlockSpec((lp, HEAD_DIM), lambda g, i: (0, g)),
                  pl.BlockSpec((VT_ROWS, lp), lambda g, i: (g, 0))],
        out_specs=pl.BlockSpec((TQ, gw), lambda g, i: (i, g)),
        out_shape=jax.ShapeDtypeStruct((lp, ATTN_W), BF16),
        scratch_shapes=[pltpu.VMEM((n_sub, 1, KV_REP * TQS), F32),
                        pltpu.VMEM((n_sub, VT_ROWS, KV_REP * TQS), F32),
                        pltpu.VMEM((2, n_sub, TK, KV_REP * TQS), BF16)],
        compiler_params=_cparams("parallel", "parallel"),
        name="attention_stabilised" if stabilised else "attention",
    )(q, k, vt)


def _s5_tables(a_re, a_im, log_dt, b_re, b_im, c_re, c_im, d_skip):
    t = S5_T
    lag = jnp.arange(t + 1, dtype=F32)
    per_dir = []
    for d in range(2):
        a = lax.complex(a_re[d], a_im[d])
        adt = a * jnp.exp(log_dt[d])[:, None]
        a_bar = jnp.exp(adt)
        b_bar = ((a_bar - 1.0) / a)[..., None] * lax.complex(b_re[d], b_im[d])
        c = lax.complex(c_re[d], c_im[d])
        pw = jnp.exp(adt[None] * lag[:, None, None].astype(jnp.complex64))
        klag = jnp.real(jnp.einsum('gon,dgn,gni->dgoi', c, pw[:t], b_bar,
                                   precision=lax.Precision.HIGHEST))
        per_dir.append((b_bar, c, pw, klag))
    (bb0, c0, pw0, kl0), (bb1, c1, pw1, kl1) = per_dir
    ii = jnp.arange(t)[:, None, None]
    jj = jnp.arange(t)[None, :, None]
    dd = jnp.arange(t)[None, None, :]
    place_f = (jj - ii == dd).astype(F32)
    place_b = (ii - jj == dd).astype(F32)
    skip = d_skip.reshape(S5_G, S5_P)[:, :, None] * jnp.eye(S5_P, dtype=F32)
    kl0 = kl0.at[0].add(skip)
    kmat = (jnp.einsum('ijd,dgop->gipjo', place_f, kl0, precision=lax.Precision.HIGHEST)
            + jnp.einsum('ijd,dgop->gipjo', place_b, kl1, precision=lax.Precision.HIGHEST)
            ).reshape(S5_G, S5_CW, S5_CW)

    bf = pw0[:t][::-1][..., None] * bb0[None]
    bb = pw1[:t][..., None] * bb1[None]
    cf = c0[None] * pw0[1:t + 1][:, :, None, :]
    cb = c1[None] * pw1[1:t + 1][::-1][:, :, None, :]

    def in_mat(x):
        return x.transpose(1, 0, 3, 2).reshape(S5_G, S5_CW, S5_N)

    def out_mat(x):
        return x.transpose(1, 3, 0, 2).reshape(S5_G, S5_N, S5_CW)

    n_pair = S5_G // 2
    zero_in = jnp.zeros((n_pair, S5_CW, S5_N), F32)
    zero_out = jnp.zeros((n_pair, S5_N, S5_CW), F32)

    def pair_in(x):
        e, o = x[0::2], x[1::2]
        return jnp.concatenate([jnp.concatenate([e, zero_in], axis=2),
                                jnp.concatenate([zero_in, o], axis=2)], axis=1)

    def pair_out(x):
        e, o = x[0::2], x[1::2]
        return jnp.concatenate([jnp.concatenate([e, zero_out], axis=2),
                                jnp.concatenate([zero_out, o], axis=2)], axis=1)

    bmat = jnp.concatenate([pair_in(in_mat(jnp.real(bf))), pair_in(in_mat(jnp.imag(bf))),
                            pair_in(in_mat(jnp.real(bb))), pair_in(in_mat(jnp.imag(bb)))], axis=2)
    cmat = jnp.concatenate([pair_out(out_mat(jnp.real(cf))), pair_out(out_mat(-jnp.imag(cf))),
                            pair_out(out_mat(jnp.real(cb))), pair_out(out_mat(-jnp.imag(cb)))], axis=1)

    def pair_row(x):
        return x.reshape(S5_G // S5_GB, 1, S5_PB * 2 * S5_N)

    a_pow = jnp.concatenate([pair_row(jnp.real(pw0[t])), pair_row(jnp.imag(pw0[t])),
                             pair_row(jnp.real(pw1[t])), pair_row(jnp.imag(pw1[t]))], axis=1)
    return kmat.astype(BF16), bmat.astype(BF16), cmat.astype(BF16), a_pow


LANE_GROUPS = 128 // S5_P


def _swap_token_group(xs):
    block = lax.broadcasted_iota(jnp.int32, xs[0].shape, 1) // S5_P
    xs = list(xs)
    d = LANE_GROUPS // 2
    while d:
        upper = (block & d) != 0
        for a in range(LANE_GROUPS):
            if a & d:
                continue
            lo, hi = xs[a], xs[a + d]
            xs[a] = jnp.where(upper, pltpu.roll(hi, S5_P * d, 1), lo)
            xs[a + d] = jnp.where(upper, hi, pltpu.roll(lo, 128 - S5_P * d, 1))
        d //= 2
    return xs


def _s5_pack_kernel(u_ref, o_ref, *, n_chunks):
    n_main = n_chunks // 16 * 16
    for half in range(S5_T // LANE_GROUPS):
        xs = [u_ref[pl.ds(LANE_GROUPS * half + i, n_chunks, stride=S5_T), :] for i in range(LANE_GROUPS)]
        for g, y in enumerate(_swap_token_group(xs)):
            cols = slice(g * S5_CW + half * 128, g * S5_CW + (half + 1) * 128)
            o_ref[0:n_main, cols] = y[0:n_main].astype(BF16)
            if o_ref.shape[0] > n_main:
                tail = jnp.concatenate([y[n_main:], jnp.zeros((o_ref.shape[0] - n_chunks, 128), F32)], axis=0)
                o_ref[n_main:, cols] = tail.astype(BF16)


def _s5_unpack_kernel(y_ref, o_ref, *, n_chunks):
    for half in range(S5_T // LANE_GROUPS):
        ys = [y_ref[:, g * S5_CW + half * 128:g * S5_CW + (half + 1) * 128].astype(F32)[0:n_chunks]
              for g in range(LANE_GROUPS)]
        for i, x in enumerate(_swap_token_group(ys)):
            o_ref[pl.ds(LANE_GROUPS * half + i, n_chunks, stride=S5_T), :] = x


def _s5_pack(u, n_chunks_pad):
    lp = u.shape[0]
    bw = LANE_GROUPS * S5_CW
    return pl.pallas_call(
        functools.partial(_s5_pack_kernel, n_chunks=lp // S5_T),
        grid=(S5_W // 128,),
        in_specs=[pl.BlockSpec((lp, 128), lambda i: (0, i))],
        out_specs=pl.BlockSpec((n_chunks_pad, bw), lambda i: (0, i)),
        out_shape=jax.ShapeDtypeStruct((n_chunks_pad, S5_G * S5_CW), BF16),
        compiler_params=_cparams("parallel"),
        name="s5_pack",
    )(u)


def _s5_unpack(yc, lp):
    bw = LANE_GROUPS * S5_CW
    return pl.pallas_call(
        functools.partial(_s5_unpack_kernel, n_chunks=lp // S5_T),
        grid=(S5_W // 128,),
        in_specs=[pl.BlockSpec((yc.shape[0], bw), lambda i: (0, i))],
        out_specs=pl.BlockSpec((lp, 128), lambda i: (0, i)),
        out_shape=jax.ShapeDtypeStruct((lp, S5_W), F32),
        compiler_params=_cparams("parallel"),
        name="s5_unpack",
    )(yc)


def _s5_kernel(u_ref, k_ref, b_ref, c_ref, a_ref, y_ref, fre, fim, bre, bim, *, n_chunks):
    pw = 2 * S5_N
    for p in range(S5_PB):
        s = _dot(u_ref[:, p * 2 * S5_CW:(p + 1) * 2 * S5_CW], b_ref[p])
        cols = slice(p * pw, (p + 1) * pw)
        fre[:, cols] = s[:, 0:pw]
        fim[:, cols] = s[:, pw:2 * pw]
        bre[:, cols] = s[:, 2 * pw:3 * pw]
        bim[:, cols] = s[:, 3 * pw:4 * pw]

    far, fai, bar, bai = a_ref[0, 0:1, :], a_ref[0, 1:2, :], a_ref[0, 2:3, :], a_ref[0, 3:4, :]

    def step(i, carry):
        xr, xi, yr, yi = carry
        rf = pl.ds(i, 1)
        rb = pl.ds(n_chunks - 1 - i, 1)
        sr, si = fre[rf, :], fim[rf, :]
        tr, ti = bre[rb, :], bim[rb, :]
        fre[rf, :] = xr
        fim[rf, :] = xi
        bre[rb, :] = yr
        bim[rb, :] = yi
        return (far * xr - fai * xi + sr, far * xi + fai * xr + si,
                bar * yr - bai * yi + tr, bar * yi + bai * yr + ti)

    zero = jnp.zeros((1, S5_PB * pw), F32)
    lax.fori_loop(0, n_chunks, step, (zero, zero, zero, zero))

    for p in range(S5_PB):
        cols = slice(p * pw, (p + 1) * pw)
        state = jnp.concatenate([fre[:, cols], fim[:, cols], bre[:, cols], bim[:, cols]], axis=1).astype(BF16)
        carry_out = _dot(state, c_ref[p])
        for e in range(2):
            g = 2 * p + e
            gc = slice(g * S5_CW, (g + 1) * S5_CW)
            y = _dot(u_ref[:, gc], k_ref[g]) + carry_out[:, e * S5_CW:(e + 1) * S5_CW]
            y_ref[:, gc] = y.astype(BF16)


def _s5(layer, u_chunks, kmat, bmat, cmat, a_pow):
    n_chunks = u_chunks.shape[0]
    bw = S5_GB * S5_CW
    sw = S5_PB * 2 * S5_N
    return pl.pallas_call(
        functools.partial(_s5_kernel, n_chunks=n_chunks),
        grid=(S5_G // S5_GB,),
        in_specs=[pl.BlockSpec((n_chunks, bw), lambda i: (0, i)),
                  pl.BlockSpec((None, S5_GB, S5_CW, S5_CW), lambda i: (layer, i, 0, 0)),
                  pl.BlockSpec((None, S5_PB, 2 * S5_CW, 8 * S5_N), lambda i: (layer, i, 0, 0)),
                  pl.BlockSpec((None, S5_PB, 8 * S5_N, 2 * S5_CW), lambda i: (layer, i, 0, 0)),
                  pl.BlockSpec((None, 1, 4, sw), lambda i: (layer, i, 0, 0))],
        out_specs=pl.BlockSpec((n_chunks, bw), lambda i: (0, i)),
        out_shape=jax.ShapeDtypeStruct(u_chunks.shape, BF16),
        scratch_shapes=[pltpu.VMEM((n_chunks, sw), F32)] * 4,
        compiler_params=_cparams("parallel"),
        name="s5_scan",
    )(u_chunks, kmat, bmat, cmat, a_pow)


def _row_keep(shape, tile_rows):
    row = pl.program_id(0) * tile_rows + lax.broadcasted_iota(jnp.int32, shape, 0)
    return row >= N_DUMMY


def _even_out_kernel(o_ref, ga_ref, y_ref, gb_ref, h_ref, wglu_ref, bglu_ref, wout_ref, gpost_ref, hout_ref):
    ya = o_ref[...].astype(F32) * _silu(ga_ref[...].astype(F32))
    yb = _gelu_tanh(y_ref[...].astype(F32))
    yb = yb * _sigmoid(_dot(yb.astype(BF16), wglu_ref[...]) + bglu_ref[...])
    yb = yb * _silu(gb_ref[...].astype(F32))
    out = _dot(ya.astype(BF16), wout_ref[0:ATTN_W, :]) + _dot(yb.astype(BF16), wout_ref[ATTN_W:, :])
    hnew = h_ref[...] + _rms(out, gpost_ref[...])
    hout_ref[...] = jnp.where(_row_keep(hnew.shape, TM), hnew, 0.0)


def _even_out(layer, i, o, ga, y, gb, hres, wglu, bglu, wout, gpost):
    lp = hres.shape[0]
    row = lambda w: pl.BlockSpec((TM, w), lambda i: (i, 0))
    _full, _full_l = _slab(i), _slab(layer)
    return pl.pallas_call(
        _even_out_kernel,
        grid=(lp // TM,),
        in_specs=[row(ATTN_W), row(ATTN_W), row(S5_W), row(S5_W), row(D_MODEL),
                  _full((S5_W, S5_W)), _full((1, S5_W)), _full((ATTN_W + S5_W, D_MODEL)), _full_l((1, D_MODEL))],
        out_specs=row(D_MODEL),
        out_shape=jax.ShapeDtypeStruct((lp, D_MODEL), F32),
        compiler_params=_cparams("parallel"),
        name="even_out",
    )(o, ga, y, gb, hres, wglu, bglu, wout, gpost)


def _odd_in_kernel(h_ref, gpre_ref, w_ref, xm_ref, z_ref):
    hn = _rms(h_ref[...], gpre_ref[...]).astype(BF16)
    xm_ref[...] = _dot(hn, w_ref[:, 0:ML_W]).astype(BF16)
    z_ref[...] = _dot(hn, w_ref[:, ML_W:]).astype(BF16)


def _odd_in(layer, i, hres, gpre, w_in):
    lp = hres.shape[0]
    row = lambda w: pl.BlockSpec((TM, w), lambda i: (i, 0))
    return pl.pallas_call(
        _odd_in_kernel,
        grid=(lp // TM,),
        in_specs=[row(D_MODEL), _slab(layer)((1, D_MODEL)), _slab(i)((D_MODEL, 2 * ML_W))],
        out_specs=[row(ML_W), row(ML_W)],
        out_shape=[jax.ShapeDtypeStruct((lp, ML_W), BF16)] * 2,
        compiler_params=_cparams("parallel"),
        name="odd_in",
    )(hres, gpre, w_in)


HALO = 8
GATE_LANES = 128


def _odd_mid_kernel(xm_ref, prev_ref, next_ref, cw_ref, cb_ref, wqk_ref, wv_ref, wg_ref, bg_ref,
                    q_ref, k_ref, v_ref, xc_ref, g_ref, ext):
    i = pl.program_id(0)
    n = pl.num_programs(0)
    xm = xm_ref[...]
    ext[0:HALO, :] = jnp.where(i > 0, prev_ref[...].astype(F32), 0.0)
    ext[HALO:HALO + TM, :] = xm.astype(F32)
    ext[HALO + TM:, :] = jnp.where(i < n - 1, next_ref[...].astype(F32), 0.0)
    conv = cb_ref[...]
    for j in range(CONV_W):
        conv = conv + cw_ref[j:j + 1, :] * ext[pl.ds(HALO - CONV_W // 2 + j, TM), :]
    xc = _silu(conv)
    xcb = xc.astype(BF16)
    xc_ref[...] = xcb

    for h in range(ML_H):
        cin = slice(h * ML_DV, (h + 1) * ML_DV)
        qk = _dot(xcb[:, cin], wqk_ref[h])
        q_ref[:, h * ML_DK:(h + 1) * ML_DK] = (qk[:, 0:ML_DK] * (ML_DK ** -0.5)).astype(BF16)
        k_ref[:, h * ML_DK:(h + 1) * ML_DK] = qk[:, ML_DK:].astype(BF16)
        v_ref[:, cin] = _dot(xm[:, cin], wv_ref[h]).astype(BF16)

    pre = _dot(xcb, wg_ref[...]) + bg_ref[...]
    lane = lax.broadcasted_iota(jnp.int32, pre.shape, 1)
    keep = _row_keep(pre.shape, TM)
    li = jnp.where(keep, pre, NEG_BIG)
    lf = jnp.where(keep, jnp.minimum(pre, 0.0) - jnp.log(1.0 + jnp.exp(-jnp.abs(pre))), 0.0)
    t = lax.broadcasted_iota(jnp.int32, (TM, TM), 0)
    s = lax.broadcasted_iota(jnp.int32, (TM, TM), 1)
    same = (t // ML_CHUNK) == (s // ML_CHUNK)
    tri_f = jnp.where(same & (s <= t), 1.0, 0.0).astype(BF16)
    tri_b = jnp.where(same & (s >= t), 1.0, 0.0).astype(BF16)
    lf_hi = lf.astype(BF16)
    lf_lo = (lf - lf_hi.astype(F32)).astype(BF16)
    cum_f = _dot(tri_f, lf_hi) + _dot(tri_f, lf_lo)
    cum_b = _dot(tri_b, lf_hi) + _dot(tri_b, lf_lo)
    g_ref[...] = jnp.where(lane < 2 * ML_H, li, jnp.where(lane < 3 * ML_H, cum_f, cum_b))


def _odd_mid(i, xm, cw, cb, wqk, wv, wg, bg):
    lp = xm.shape[0]
    row = lambda w: pl.BlockSpec((TM, w), lambda i: (i, 0))
    _full = _slab(i)
    per = TM // HALO
    last = lp // HALO - 1
    return pl.pallas_call(
        _odd_mid_kernel,
        grid=(lp // TM,),
        in_specs=[row(ML_W),
                  pl.BlockSpec((HALO, ML_W), lambda i: (jnp.maximum(i * per - 1, 0), 0)),
                  pl.BlockSpec((HALO, ML_W), lambda i: (jnp.minimum((i + 1) * per, last), 0)),
                  _full((CONV_W, ML_W)), _full((1, ML_W)),
                  _full((ML_H, ML_DV, 2 * ML_DK)), _full((ML_H, ML_DV, ML_DV)),
                  _full((ML_W, GATE_LANES)), _full((1, GATE_LANES))],
        out_specs=[row(ML_H * ML_DK), row(ML_H * ML_DK), row(ML_W), row(ML_W), row(GATE_LANES)],
        out_shape=[jax.ShapeDtypeStruct((lp, ML_H * ML_DK), BF16), jax.ShapeDtypeStruct((lp, ML_H * ML_DK), BF16),
                   jax.ShapeDtypeStruct((lp, ML_W), BF16), jax.ShapeDtypeStruct((lp, ML_W), BF16),
                   jax.ShapeDtypeStruct((lp, GATE_LANES), F32)],
        scratch_shapes=[pltpu.VMEM((TM + 2 * HALO, ML_W), F32)],
        compiler_params=_cparams("parallel"),
        name="odd_mid",
    )(xm, xm, xm, cw, cb, wqk, wv, wg, bg)


def _mlstm_chunk(q, k, v, g, gt, h, backward, c_sc, m_sc, slot):
    c = ML_CHUNK
    off = ML_H if backward else 0
    li_row = gt[off + h:off + h + 1, :]
    b_row = gt[2 * ML_H + off + h:2 * ML_H + off + h + 1, :]
    a_row = li_row - b_row
    a_col = g[:, off + h:off + h + 1] - g[:, 2 * ML_H + off + h:2 * ML_H + off + h + 1]
    b_last = jnp.broadcast_to(b_row[:, 0:1] if backward else b_row[:, c - 1:c], (1, c))
    m_prev = m_sc[slot]
    state = c_sc[slot]

    kq = _dot_nt(k, q)
    vt_aug = jnp.concatenate([v.T, jnp.ones((ML_AUG, c), BF16)], axis=0)
    qt = q.T.astype(F32)
    m_new = b_last + jnp.maximum(m_prev, jnp.max(a_row, axis=-1, keepdims=True))
    decay = jnp.exp(b_last + m_prev - m_new)
    w_row = jnp.exp(b_last + a_row - m_new)
    update = _dot((vt_aug.astype(F32) * w_row).astype(BF16), k)
    yield None

    s = lax.broadcasted_iota(jnp.int32, (c, c), 0)
    t = lax.broadcasted_iota(jnp.int32, (c, c), 1)
    allowed = (s >= t) if backward else (s <= t)
    dt = jnp.where(allowed, b_row + a_col, NEG)
    inter_log = b_row + m_prev
    m_t = jnp.maximum(jnp.max(dt, axis=0, keepdims=True), inter_log)
    swt = jnp.exp(dt - m_t) * kq
    inter = jnp.exp(inter_log - m_t)
    num = _dot(jnp.concatenate([vt_aug, state.astype(BF16)], axis=1),
               jnp.concatenate([swt.astype(BF16), (qt * inter).astype(BF16)], axis=0))
    yield None

    den = num[ML_DV:ML_DV + 1, :]
    out_t = num[0:ML_DV, :] / jnp.maximum(jnp.abs(den), jnp.exp(-m_t))
    c_sc[slot] = decay * state + update
    m_sc[slot] = m_new
    yield out_t.T


def _mlstm_kernel(qf_ref, kf_ref, vf_ref, gf_ref, qb_ref, kb_ref, vb_ref, gb_ref, hf_ref, hb_ref, c_sc, m_sc):
    @pl.when(pl.program_id(0) == 0)
    def _():
        c_sc[...] = jnp.zeros(c_sc.shape, F32)
        m_sc[...] = jnp.zeros(m_sc.shape, F32)

    chunks = []
    for backward, (q_ref, k_ref, v_ref, g_ref, h_ref) in enumerate(
            ((qf_ref, kf_ref, vf_ref, gf_ref, hf_ref), (qb_ref, kb_ref, vb_ref, gb_ref, hb_ref))):
        g = g_ref[...]
        gt = g.T
        for h in range(ML_H):
            gen = _mlstm_chunk(q_ref[:, h * ML_DK:(h + 1) * ML_DK], k_ref[:, h * ML_DK:(h + 1) * ML_DK],
                               v_ref[:, h * ML_DV:(h + 1) * ML_DV], g, gt, h, bool(backward),
                               c_sc, m_sc, backward * ML_H + h)
            chunks.append((gen, h_ref, h))
    for gen, _, _ in chunks:
        next(gen)
    for gen, _, _ in chunks:
        next(gen)
    for gen, h_ref, h in chunks:
        h_ref[:, h * ML_DV:(h + 1) * ML_DV] = next(gen).astype(BF16)


def _mlstm(q, k, v, gates):
    lp = q.shape[0]
    nch = lp // ML_CHUNK
    fwd = lambda w: pl.BlockSpec((ML_CHUNK, w), lambda i: (i, 0))
    bwd = lambda w: pl.BlockSpec((ML_CHUNK, w), lambda i: (nch - 1 - i, 0))
    qk_w = ML_H * ML_DK
    return pl.pallas_call(
        _mlstm_kernel,
        grid=(nch,),
        in_specs=[fwd(qk_w), fwd(qk_w), fwd(ML_W), fwd(GATE_LANES),
                  bwd(qk_w), bwd(qk_w), bwd(ML_W), bwd(GATE_LANES)],
        out_specs=[fwd(ML_W), bwd(ML_W)],
        out_shape=[jax.ShapeDtypeStruct((lp, ML_W), BF16)] * 2,
        scratch_shapes=[pltpu.VMEM((2 * ML_H, ML_DV + ML_AUG, ML_DK), F32),
                        pltpu.VMEM((2 * ML_H, 1, ML_CHUNK), F32)],
        compiler_params=_cparams("arbitrary"),
        name="mlstm",
    )(q, k, v, gates, q, k, v, gates)


def _odd_out_kernel(hf_ref, hb_ref, xc_ref, z_ref, h_ref, wo_ref, bo_ref, hn_ref, sk_ref, wout_ref, gpost_ref,
                    hout_ref):
    xcb = xc_ref[...]
    xc = xcb.astype(F32)
    og = _sigmoid(_dot(xcb, wo_ref[...]) + bo_ref[...])
    cell = hf_ref[...].astype(F32) + hb_ref[...].astype(F32)
    parts = []
    for h in range(ML_H):
        sl = slice(h * ML_DV, (h + 1) * ML_DV)
        parts.append(_rms(cell[:, sl], hn_ref[:, sl]))
    cell = jnp.concatenate(parts, axis=1)
    hh = (og * cell + sk_ref[...] * xc) * _silu(z_ref[...].astype(F32))
    out = _dot(hh.astype(BF16), wout_ref[...])
    hnew = h_ref[...] + _rms(out, gpost_ref[...])
    hout_ref[...] = jnp.where(_row_keep(hnew.shape, TM), hnew, 0.0)


def _odd_out(layer, i, hf, hb, xc, z, hres, wo, bo, hn, sk, wout, gpost):
    lp = hres.shape[0]
    row = lambda w: pl.BlockSpec((TM, w), lambda i: (i, 0))
    _full, _full_l = _slab(i), _slab(layer)
    return pl.pallas_call(
        _odd_out_kernel,
        grid=(lp // TM,),
        in_specs=[row(ML_W), row(ML_W), row(ML_W), row(ML_W), row(D_MODEL),
                  _full((ML_W, ML_W)), _full((1, ML_W)), _full((1, ML_W)), _full((1, ML_W)),
                  _full((ML_W, D_MODEL)), _full_l((1, D_MODEL))],
        out_specs=row(D_MODEL),
        out_shape=jax.ShapeDtypeStruct((lp, D_MODEL), F32),
        compiler_params=_cparams("parallel"),
        name="odd_out",
    )(hf, hb, xc, z, hres, wo, bo, hn, sk, wout, gpost)


def _rope_tables(n_real):
    rows = n_real // GRID_W
    freqs = ROPE_THETA ** (-jnp.arange(0, ROPE_AXIS, 2, dtype=F32) / ROPE_AXIS)
    half = freqs.shape[0]
    ang_r = jnp.arange(rows, dtype=F32)[:, None] * freqs[None]
    ang_c = jnp.arange(GRID_W, dtype=F32)[:, None] * freqs[None]
    per_row = lambda t: jnp.broadcast_to(t[:, None, :], (rows, GRID_W, half)).reshape(n_real, half)
    per_col = lambda t: jnp.broadcast_to(t[None, :, :], (rows, GRID_W, half)).reshape(n_real, half)
    cr, sr = per_row(jnp.cos(ang_r)), per_row(jnp.sin(ang_r))
    cc, sc = per_col(jnp.cos(ang_c)), per_col(jnp.sin(ang_c))
    cos = jnp.concatenate([cr, cr, cc, cc], axis=1)
    sin = jnp.concatenate([-sr, sr, -sc, sc], axis=1)
    cos = jnp.concatenate([jnp.ones((FRONT, HEAD_DIM), F32), cos], axis=0)
    sin = jnp.concatenate([jnp.zeros((FRONT, HEAD_DIM), F32), sin], axis=0)
    return cos, sin


def kernel(x, meta_tokens, norm_pre, norm_post, w_in_even, q_norm, k_norm, ssm_a_re, ssm_a_im, ssm_log_dt, ssm_b_re, ssm_b_im, ssm_c_re, ssm_c_im, ssm_d, w_glu, b_glu, w_out_even, w_in_odd, conv_w, conv_b, w_q, w_k, w_v, w_igate, b_igate, w_fgate, b_fgate, w_ogate, b_ogate, head_norm, mlstm_skip, w_out_odd):
    bsz, n_real, _ = x.shape
    assert bsz == 1 and n_real % TK == 0 and (n_real + FRONT) % TM == 0
    lp = n_real + FRONT
    depth = norm_pre.shape[0]
    hres = jnp.concatenate([jnp.zeros((N_DUMMY, D_MODEL), F32), meta_tokens.astype(F32), x[0]], axis=0)
    cos, sin = _rope_tables(n_real)
    n_chunks_pad = -(-(lp // S5_T) // 16) * 16

    bf = lambda a: a.astype(BF16)
    vec = lambda a: a.reshape(a.shape[0], 1, -1)
    norm_pre, norm_post = vec(norm_pre), vec(norm_post)
    w_in_even = bf(w_in_even)
    wvt = jnp.swapaxes(w_in_even[:, :, ATTN_W + KV_W:ATTN_W + 2 * KV_W], 1, 2)
    qg = vec(q_norm) * (HEAD_DIM ** -0.5 * math.log2(math.e))
    kmat, bmat, cmat, a_pow = jax.vmap(_s5_tables)(ssm_a_re, ssm_a_im, ssm_log_dt, ssm_b_re, ssm_b_im,
                                                   ssm_c_re, ssm_c_im, ssm_d)
    w_glu, w_out_even = bf(w_glu), bf(w_out_even)
    w_in_odd, w_ogate, w_out_odd, w_v = bf(w_in_odd), bf(w_ogate), bf(w_out_odd), bf(w_v)
    wqk = bf(jnp.concatenate([w_q, w_k], axis=3))
    gate_pad = GATE_LANES - 4 * ML_H
    wg = jnp.concatenate([w_igate[:, 0], w_igate[:, 1], w_fgate[:, 0], w_fgate[:, 1]], axis=2)
    wg = bf(jnp.pad(wg, ((0, 0), (0, 0), (0, gate_pad))))
    bg = jnp.concatenate([b_igate[:, 0], b_igate[:, 1], b_fgate[:, 0], b_fgate[:, 1]], axis=1)
    bg = vec(jnp.pad(bg, ((0, 0), (0, gate_pad))))

    for layer in range(depth):
        i = layer // 2
        if layer % 2 == 0:
            q, k, vt, ga, u, gb = _even_in(layer, i, hres, norm_pre, w_in_even, wvt, qg, vec(k_norm), cos, sin)
            logit_bound = (math.sqrt(HEAD_DIM) * math.log2(math.e)
                           * jnp.max(jnp.abs(q_norm[i])) * jnp.max(jnp.abs(k_norm[i])))
            o = lax.cond(logit_bound <= LOGIT_SAFE,
                         lambda q, k, vt: _attention(q, k, vt, False),
                         lambda q, k, vt: _attention(q, k, vt, True), q, k, vt)
            y = _s5_unpack(_s5(i, _s5_pack(u, n_chunks_pad), kmat, bmat, cmat, a_pow), lp)
            hres = _even_out(layer, i, o, ga, y, gb, hres, w_glu, vec(b_glu), w_out_even, norm_post)
        else:
            xm, z = _odd_in(layer, i, hres, norm_pre, w_in_odd)
            qm, km, vm, xc, gates = _odd_mid(i, xm, conv_w, vec(conv_b), wqk, w_v, wg, bg)
            hf, hb = _mlstm(qm, km, vm, gates)
            hres = _odd_out(layer, i, hf, hb, xc, z, hres, w_ogate, vec(b_ogate), vec(head_norm), vec(mlstm_skip),
                            w_out_odd, norm_post)
    return hres[FRONT:].reshape(1, n_real, D_MODEL)
```

```python
import functools
import math

import jax
import jax.numpy as jnp
from jax import lax
from jax.experimental import pallas as pl
from jax.experimental.pallas import tpu as pltpu

F32 = jnp.float32
BF16 = jnp.bfloat16

D_MODEL = 1024
N_META = 16
FRONT = 128
N_DUMMY = FRONT - N_META
GRID_W = 64
EPS = 1e-6
NEG = -1e30

HEADS = 8
KV_HEADS = 2
KV_REP = HEADS // KV_HEADS
HEAD_DIM = 128
ROPE_AXIS = HEAD_DIM // 2
ROPE_THETA = 10000.0
ATTN_W = HEADS * HEAD_DIM
KV_W = KV_HEADS * HEAD_DIM
TQ = 384
TQS = 128
TK = 4096
VT_ROWS = HEAD_DIM + 16
LOGIT_SAFE = 60.0

S5_W = 1024
S5_P = 16
S5_G = S5_W // S5_P
S5_N = 64
S5_T = 16
S5_GB = 8
S5_PB = S5_GB // 2
S5_CW = S5_T * S5_P

ML_W = 2048
ML_H = 8
ML_DV = 256
ML_DK = 128
CONV_W = 5
ML_CHUNK = 128
ML_AUG = 16
NEG_BIG = -1e9

TM = 384
VMEM_LIMIT = 56 * 1024 * 1024


def _cparams(*sem):
    return pltpu.CompilerParams(dimension_semantics=sem, vmem_limit_bytes=VMEM_LIMIT)


def _rms(x, g):
    return x * lax.rsqrt(jnp.mean(x * x, axis=-1, keepdims=True) + EPS) * g


def _sigmoid(x):
    return 1.0 / (1.0 + jnp.exp(-x))


def _silu(x):
    return x * _sigmoid(x)


def _gelu_tanh(x):
    return 0.5 * x * (1.0 + jnp.tanh(math.sqrt(2.0 / math.pi) * (x + 0.044715 * (x * x * x))))


def _dot(a, b):
    return jnp.dot(a, b, preferred_element_type=F32)


def _dot_nt(a, b):
    return lax.dot_general(a, b, (((1,), (1,)), ((), ())), preferred_element_type=F32)


def _dot_tn(a, b):
    return lax.dot_general(a, b, (((0,), (0,)), ((), ())), preferred_element_type=F32)


def _slab(layer):
    def spec(shape):
        return pl.BlockSpec((None,) + tuple(shape), lambda *_: (layer,) + (0,) * len(shape))
    return spec


def _even_in_kernel(h_ref, gpre_ref, w_ref, wvt_ref, qg_ref, kg_ref, cos_ref, sin_ref,
                    q_ref, k_ref, vt_ref, ga_ref, u_ref, gb_ref):
    hn = _rms(h_ref[...], gpre_ref[...]).astype(BF16)
    cos = cos_ref[...]
    sin = sin_ref[...]
    lane = lax.broadcasted_iota(jnp.int32, cos.shape, 1)
    first_half = (lane % ROPE_AXIS) < (ROPE_AXIS // 2)

    def head(x, g):
        y = _rms(x, g)
        partner = jnp.where(first_half, pltpu.roll(y, HEAD_DIM - ROPE_AXIS // 2, 1),
                            pltpu.roll(y, ROPE_AXIS // 2, 1))
        return (y * cos + partner * sin).astype(BF16)

    qsec = _dot(hn, w_ref[:, 0:ATTN_W])
    for h in range(HEADS):
        sl = slice(h * HEAD_DIM, (h + 1) * HEAD_DIM)
        q_ref[:, sl] = head(qsec[:, sl], qg_ref[...])
    ksec = _dot(hn, w_ref[:, ATTN_W:ATTN_W + KV_W])
    for h in range(KV_HEADS):
        sl = slice(h * HEAD_DIM, (h + 1) * HEAD_DIM)
        k_ref[:, sl] = head(ksec[:, sl], kg_ref[...])
    vt = _dot_nt(wvt_ref[...], hn).astype(BF16)
    for g in range(KV_HEADS):
        vt_ref[g * VT_ROWS:g * VT_ROWS + HEAD_DIM, :] = vt[g * HEAD_DIM:(g + 1) * HEAD_DIM]
        vt_ref[g * VT_ROWS + HEAD_DIM:(g + 1) * VT_ROWS, :] = jnp.ones((VT_ROWS - HEAD_DIM, TM), BF16)
    c0 = ATTN_W + 2 * KV_W
    ga_ref[...] = _dot(hn, w_ref[:, c0:c0 + ATTN_W]).astype(BF16)
    c0 += ATTN_W
    u_ref[...] = _dot(hn, w_ref[:, c0:c0 + S5_W])
    c0 += S5_W
    gb_ref[...] = _dot(hn, w_ref[:, c0:c0 + S5_W]).astype(BF16)


def _even_in(layer, i, hres, gpre, w_in, wvt, qg, kg, cos, sin):
    lp = hres.shape[0]
    win_w = w_in.shape[-1]
    _full, _full_l = _slab(i), _slab(layer)
    row = lambda w: pl.BlockSpec((TM, w), lambda i: (i, 0))
    vt_rows = KV_HEADS * VT_ROWS
    col = pl.BlockSpec((vt_rows, TM), lambda i: (0, i))
    widths = [ATTN_W, KV_W, None, ATTN_W, S5_W, S5_W]
    return pl.pallas_call(
        _even_in_kernel,
        grid=(lp // TM,),
        in_specs=[row(D_MODEL), _full_l((1, D_MODEL)), _full((D_MODEL, win_w)), _full((KV_W, D_MODEL)),
                  _full((1, HEAD_DIM)), _full((1, HEAD_DIM)), row(HEAD_DIM), row(HEAD_DIM)],
        out_specs=[col if w is None else row(w) for w in widths],
        out_shape=[jax.ShapeDtypeStruct((vt_rows, lp) if w is None else (lp, w), F32 if n == 4 else BF16)
                   for n, w in enumerate(widths)],
        compiler_params=_cparams("parallel"),
        name="even_in",
    )(hres, gpre, w_in, wvt, qg, kg, cos, sin)


def _attn_kernel(q_ref, k_ref, vt_ref, o_ref, m_sc, acc_sc, p_sc, *, n_kv_tiles, stabilised):
    qs = [jnp.concatenate([q_ref[a * TQS:(a + 1) * TQS, r * HEAD_DIM:(r + 1) * HEAD_DIM] for r in range(KV_REP)],
                          axis=0) for a in range(TQ // TQS)]

    def scores(kb, a, mask_dummy=False):
        st = _dot_nt(kb, qs[a])
        if mask_dummy:
            key = lax.broadcasted_iota(jnp.int32, st.shape, 0)
            st = jnp.where(key >= N_DUMMY, st, NEG)
        return st

    def tile_start(j):
        return pl.multiple_of(FRONT + j * TK, 128)

    k0, vt0 = k_ref[0:FRONT, :], vt_ref[:, 0:FRONT]
    if stabilised:
        def update(kb, vtb, mask_dummy):
            for a in range(len(qs)):
                st = scores(kb, a, mask_dummy)
                m_old = m_sc[a]
                m_new = jnp.maximum(m_old, jnp.max(st, axis=0, keepdims=True))
                p = jnp.exp2(st - m_new).astype(BF16)
                acc_sc[a] = jnp.exp2(m_old - m_new) * acc_sc[a] + _dot(vtb, p)
                m_sc[a] = m_new

        m_sc[...] = jnp.full(m_sc.shape, NEG, F32)
        acc_sc[...] = jnp.zeros(acc_sc.shape, F32)
        update(k0, vt0, True)

        def body(j, carry):
            update(k_ref[pl.ds(tile_start(j), TK), :], vt_ref[:, pl.ds(tile_start(j), TK)], False)
            return carry

        lax.fori_loop(0, n_kv_tiles, body, 0)
    else:
        for a in range(len(qs)):
            acc_sc[a] = _dot(vt0, jnp.exp2(scores(k0, a, True).astype(BF16)))
            p_sc[0, a] = jnp.exp2(scores(k_ref[FRONT:FRONT + TK, :], a).astype(BF16))

        def body(j, carry):
            cur, prev = j % 2, (j + 1) % 2
            kb = k_ref[pl.ds(tile_start(j), TK), :]
            vtb = vt_ref[:, pl.ds(tile_start(j - 1), TK)]
            for a in range(len(qs)):
                acc_sc[a] += _dot(vtb, p_sc[prev, a])
            for a in range(len(qs)):
                p_sc[cur, a] = jnp.exp2(scores(kb, a).astype(BF16))
            return carry

        lax.fori_loop(1, n_kv_tiles, body, 0)
        last = (n_kv_tiles - 1) % 2
        vtb = vt_ref[:, FRONT + (n_kv_tiles - 1) * TK:FRONT + n_kv_tiles * TK]
        for a in range(len(qs)):
            acc_sc[a] += _dot(vtb, p_sc[last, a])

    for a in range(TQ // TQS):
        ot = acc_sc[a, 0:HEAD_DIM, :] / acc_sc[a, HEAD_DIM:HEAD_DIM + 1, :]
        for r in range(KV_REP):
            o_ref[a * TQS:(a + 1) * TQS, r * HEAD_DIM:(r + 1) * HEAD_DIM] = (
                ot[:, r * TQS:(r + 1) * TQS].T.astype(BF16))


def _attention(q, k, vt, stabilised):
    lp = q.shape[0]
    n_kv_tiles = (lp - FRONT) // TK
    gw = KV_REP * HEAD_DIM
    n_sub = TQ // TQS
    return pl.pallas_call(
        functools.partial(_attn_kernel, n_kv_tiles=n_kv_tiles, stabilised=stabilised),
        grid=(KV_HEADS, lp // TQ),
        in_specs=[pl.BlockSpec((TQ, gw), lambda g, i: (i, g)),
                  pl.BlockSpec((lp, HEAD_DIM), lambda g, i: (0, g)),
                  pl.BlockSpec((VT_ROWS, lp), lambda g, i: (g, 0))],
        out_specs=pl.BlockSpec((TQ, gw), lambda g, i: (i, g)),
        out_shape=jax.ShapeDtypeStruct((lp, ATTN_W), BF16),
        scratch_shapes=[pltpu.VMEM((n_sub, 1, KV_REP * TQS), F32),
                        pltpu.VMEM((n_sub, VT_ROWS, KV_REP * TQS), F32),
                        pltpu.VMEM((2, n_sub, TK, KV_REP * TQS), BF16)],
        compiler_params=_cparams("parallel", "parallel"),
        name="attention_stabilised" if stabilised else "attention",
    )(q, k, vt)


def _s5_tables(a_re, a_im, log_dt, b_re, b_im, c_re, c_im, d_skip):
    t = S5_T
    lag = jnp.arange(t + 1, dtype=F32)
    per_dir = []
    for d in range(2):
        a = lax.complex(a_re[d], a_im[d])
        adt = a * jnp.exp(log_dt[d])[:, None]
        a_bar = jnp.exp(adt)
        b_bar = ((a_bar - 1.0) / a)[..., None] * lax.complex(b_re[d], b_im[d])
        c = lax.complex(c_re[d], c_im[d])
        pw = jnp.exp(adt[None] * lag[:, None, None].astype(jnp.complex64))
        klag = jnp.real(jnp.einsum('gon,dgn,gni->dgoi', c, pw[:t], b_bar,
                                   precision=lax.Precision.HIGHEST))
        per_dir.append((b_bar, c, pw, klag))
    (bb0, c0, pw0, kl0), (bb1, c1, pw1, kl1) = per_dir
    ii = jnp.arange(t)[:, None, None]
    jj = jnp.arange(t)[None, :, None]
    dd = jnp.arange(t)[None, None, :]
    place_f = (jj - ii == dd).astype(F32)
    place_b = (ii - jj == dd).astype(F32)
    skip = d_skip.reshape(S5_G, S5_P)[:, :, None] * jnp.eye(S5_P, dtype=F32)
    kl0 = kl0.at[0].add(skip)
    kmat = (jnp.einsum('ijd,dgop->gipjo', place_f, kl0, precision=lax.Precision.HIGHEST)
            + jnp.einsum('ijd,dgop->gipjo', place_b, kl1, precision=lax.Precision.HIGHEST)
            ).reshape(S5_G, S5_CW, S5_CW)

    bf = pw0[:t][::-1][..., None] * bb0[None]
    bb = pw1[:t][..., None] * bb1[None]
    cf = c0[None] * pw0[1:t + 1][:, :, None, :]
    cb = c1[None] * pw1[1:t + 1][::-1][:, :, None, :]

    def in_mat(x):
        return x.astype(BF16).transpose(1, 0, 3, 2).reshape(S5_G, S5_CW, S5_N)

    def out_mat(x):
        return x.astype(BF16).transpose(1, 3, 0, 2).reshape(S5_G, S5_N, S5_CW)

    n_pair = S5_G // 2
    zero_in = jnp.zeros((n_pair, S5_CW, S5_N), BF16)
    zero_out = jnp.zeros((n_pair, S5_N, S5_CW), BF16)

    def pair_in(x):
        e, o = x[0::2], x[1::2]
        return jnp.concatenate([jnp.concatenate([e, zero_in], axis=2),
                                jnp.concatenate([zero_in, o], axis=2)], axis=1)

    def pair_out(x):
        e, o = x[0::2], x[1::2]
        return jnp.concatenate([jnp.concatenate([e, zero_out], axis=2),
                                jnp.concatenate([zero_out, o], axis=2)], axis=1)

    bmat = jnp.concatenate([pair_in(in_mat(jnp.real(bf))), pair_in(in_mat(jnp.imag(bf))),
                            pair_in(in_mat(jnp.real(bb))), pair_in(in_mat(jnp.imag(bb)))], axis=2)
    cmat = jnp.concatenate([pair_out(out_mat(jnp.real(cf))), pair_out(out_mat(-jnp.imag(cf))),
                            pair_out(out_mat(jnp.real(cb))), pair_out(out_mat(-jnp.imag(cb)))], axis=1)

    def pair_row(x):
        return x.reshape(S5_G // S5_GB, 1, S5_PB * 2 * S5_N)

    a_pow = jnp.concatenate([pair_row(jnp.real(pw0[t])), pair_row(jnp.imag(pw0[t])),
                             pair_row(jnp.real(pw1[t])), pair_row(jnp.imag(pw1[t]))], axis=1)
    return kmat.astype(BF16), bmat, cmat, a_pow


LANE_GROUPS = 128 // S5_P


def _swap_token_group(xs):
    block = lax.broadcasted_iota(jnp.int32, xs[0].shape, 1) // S5_P
    xs = list(xs)
    d = LANE_GROUPS // 2
    while d:
        upper = (block & d) != 0
        for a in range(LANE_GROUPS):
            if a & d:
                continue
            lo, hi = xs[a], xs[a + d]
            xs[a] = jnp.where(upper, pltpu.roll(hi, S5_P * d, 1), lo)
            xs[a + d] = jnp.where(upper, hi, pltpu.roll(lo, 128 - S5_P * d, 1))
        d //= 2
    return xs


def _s5_pack_kernel(u_ref, o_ref, *, n_chunks):
    n_main = n_chunks // 16 * 16
    for half in range(S5_T // LANE_GROUPS):
        xs = [u_ref[pl.ds(LANE_GROUPS * half + i, n_chunks, stride=S5_T), :] for i in range(LANE_GROUPS)]
        for g, y in enumerate(_swap_token_group(xs)):
            cols = slice(g * S5_CW + half * 128, g * S5_CW + (half + 1) * 128)
            o_ref[0:n_main, cols] = y[0:n_main].astype(BF16)
            if o_ref.shape[0] > n_main:
                tail = jnp.concatenate([y[n_main:], jnp.zeros((o_ref.shape[0] - n_chunks, 128), F32)], axis=0)
                o_ref[n_main:, cols] = tail.astype(BF16)


def _s5_unpack_kernel(y_ref, o_ref, *, n_chunks):
    for half in range(S5_T // LANE_GROUPS):
        ys = [y_ref[:, g * S5_CW + half * 128:g * S5_CW + (half + 1) * 128].astype(F32)[0:n_chunks]
              for g in range(LANE_GROUPS)]
        for i, x in enumerate(_swap_token_group(ys)):
            o_ref[pl.ds(LANE_GROUPS * half + i, n_chunks, stride=S5_T), :] = x


def _s5_pack(u, n_chunks_pad):
    lp = u.shape[0]
    bw = LANE_GROUPS * S5_CW
    return pl.pallas_call(
        functools.partial(_s5_pack_kernel, n_chunks=lp // S5_T),
        grid=(S5_W // 128,),
        in_specs=[pl.BlockSpec((lp, 128), lambda i: (0, i))],
        out_specs=pl.BlockSpec((n_chunks_pad, bw), lambda i: (0, i)),
        out_shape=jax.ShapeDtypeStruct((n_chunks_pad, S5_G * S5_CW), BF16),
        compiler_params=_cparams("parallel"),
        name="s5_pack",
    )(u)


def _s5_unpack(yc, lp):
    bw = LANE_GROUPS * S5_CW
    return pl.pallas_call(
        functools.partial(_s5_unpack_kernel, n_chunks=lp // S5_T),
        grid=(S5_W // 128,),
        in_specs=[pl.BlockSpec((yc.shape[0], bw), lambda i: (0, i))],
        out_specs=pl.BlockSpec((lp, 128), lambda i: (0, i)),
        out_shape=jax.ShapeDtypeStruct((lp, S5_W), F32),
        compiler_params=_cparams("parallel"),
        name="s5_unpack",
    )(yc)


def _s5_kernel(u_ref, k_ref, b_ref, c_ref, a_ref, y_ref, fre, fim, bre, bim, *, n_chunks):
    pw = 2 * S5_N
    for p in range(S5_PB):
        s = _dot(u_ref[:, p * 2 * S5_CW:(p + 1) * 2 * S5_CW], b_ref[p])
        cols = slice(p * pw, (p + 1) * pw)
        fre[:, cols] = s[:, 0:pw]
        fim[:, cols] = s[:, pw:2 * pw]
        bre[:, cols] = s[:, 2 * pw:3 * pw]
        bim[:, cols] = s[:, 3 * pw:4 * pw]

    far, fai, bar, bai = a_ref[0, 0:1, :], a_ref[0, 1:2, :], a_ref[0, 2:3, :], a_ref[0, 3:4, :]

    def step(i, carry):
        xr, xi, yr, yi = carry
        rf = pl.ds(i, 1)
        rb = pl.ds(n_chunks - 1 - i, 1)
        sr, si = fre[rf, :], fim[rf, :]
        tr, ti = bre[rb, :], bim[rb, :]
        fre[rf, :] = xr
        fim[rf, :] = xi
        bre[rb, :] = yr
        bim[rb, :] = yi
        return (far * xr - fai * xi + sr, far * xi + fai * xr + si,
                bar * yr - bai * yi + tr, bar * yi + bai * yr + ti)

    zero = jnp.zeros((1, S5_PB * pw), F32)
    lax.fori_loop(0, n_chunks, step, (zero, zero, zero, zero))

    for p in range(S5_PB):
        cols = slice(p * pw, (p + 1) * pw)
        state = jnp.concatenate([fre[:, cols], fim[:, cols], bre[:, cols], bim[:, cols]], axis=1).astype(BF16)
        carry_out = _dot(state, c_ref[p])
        for e in range(2):
            g = 2 * p + e
            gc = slice(g * S5_CW, (g + 1) * S5_CW)
            y = _dot(u_ref[:, gc], k_ref[g]) + carry_out[:, e * S5_CW:(e + 1) * S5_CW]
            y_ref[:, gc] = y.astype(BF16)


def _s5(layer, u_chunks, kmat, bmat, cmat, a_pow):
    n_chunks = u_chunks.shape[0]
    bw = S5_GB * S5_CW
    sw = S5_PB * 2 * S5_N
    return pl.pallas_call(
        functools.partial(_s5_kernel, n_chunks=n_chunks),
        grid=(S5_G // S5_GB,),
        in_specs=[pl.BlockSpec((n_chunks, bw), lambda i: (0, i)),
                  pl.BlockSpec((None, S5_GB, S5_CW, S5_CW), lambda i: (layer, i, 0, 0)),
                  pl.BlockSpec((None, S5_PB, 2 * S5_CW, 8 * S5_N), lambda i: (layer, i, 0, 0)),
                  pl.BlockSpec((None, S5_PB, 8 * S5_N, 2 * S5_CW), lambda i: (layer, i, 0, 0)),
                  pl.BlockSpec((None, 1, 4, sw), lambda i: (layer, i, 0, 0))],
        out_specs=pl.BlockSpec((n_chunks, bw), lambda i: (0, i)),
        out_shape=jax.ShapeDtypeStruct(u_chunks.shape, BF16),
        scratch_shapes=[pltpu.VMEM((n_chunks, sw), F32)] * 4,
        compiler_params=_cparams("parallel"),
        name="s5_scan",
    )(u_chunks, kmat, bmat, cmat, a_pow)


def _row_keep(shape, tile_rows):
    row = pl.program_id(0) * tile_rows + lax.broadcasted_iota(jnp.int32, shape, 0)
    return row >= N_DUMMY


def _even_out_kernel(o_ref, ga_ref, y_ref, gb_ref, h_ref, wglu_ref, bglu_ref, wout_ref, gpost_ref, hout_ref):
    ya = o_ref[...].astype(F32) * _silu(ga_ref[...].astype(F32))
    yb = _gelu_tanh(y_ref[...].astype(F32))
    yb = yb * _sigmoid(_dot(yb.astype(BF16), wglu_ref[...]) + bglu_ref[...])
    yb = yb * _silu(gb_ref[...].astype(F32))
    out = _dot(ya.astype(BF16), wout_ref[0:ATTN_W, :]) + _dot(yb.astype(BF16), wout_ref[ATTN_W:, :])
    hnew = h_ref[...] + _rms(out, gpost_ref[...])
    hout_ref[...] = jnp.where(_row_keep(hnew.shape, TM), hnew, 0.0)


def _even_out(layer, i, o, ga, y, gb, hres, wglu, bglu, wout, gpost):
    lp = hres.shape[0]
    row = lambda w: pl.BlockSpec((TM, w), lambda i: (i, 0))
    _full, _full_l = _slab(i), _slab(layer)
    return pl.pallas_call(
        _even_out_kernel,
        grid=(lp // TM,),
        in_specs=[row(ATTN_W), row(ATTN_W), row(S5_W), row(S5_W), row(D_MODEL),
                  _full((S5_W, S5_W)), _full((1, S5_W)), _full((ATTN_W + S5_W, D_MODEL)), _full_l((1, D_MODEL))],
        out_specs=row(D_MODEL),
        out_shape=jax.ShapeDtypeStruct((lp, D_MODEL), F32),
        compiler_params=_cparams("parallel"),
        name="even_out",
    )(o, ga, y, gb, hres, wglu, bglu, wout, gpost)


def _odd_in_kernel(h_ref, gpre_ref, w_ref, xm_ref, z_ref):
    hn = _rms(h_ref[...], gpre_ref[...]).astype(BF16)
    xm_ref[...] = _dot(hn, w_ref[:, 0:ML_W]).astype(BF16)
    z_ref[...] = _dot(hn, w_ref[:, ML_W:]).astype(BF16)


def _odd_in(layer, i, hres, gpre, w_in):
    lp = hres.shape[0]
    row = lambda w: pl.BlockSpec((TM, w), lambda i: (i, 0))
    return pl.pallas_call(
        _odd_in_kernel,
        grid=(lp // TM,),
        in_specs=[row(D_MODEL), _slab(layer)((1, D_MODEL)), _slab(i)((D_MODEL, 2 * ML_W))],
        out_specs=[row(ML_W), row(ML_W)],
        out_shape=[jax.ShapeDtypeStruct((lp, ML_W), BF16)] * 2,
        compiler_params=_cparams("parallel"),
        name="odd_in",
    )(hres, gpre, w_in)


HALO = 8
GATE_LANES = 128


def _odd_mid_kernel(xm_ref, prev_ref, next_ref, cw_ref, cb_ref, wqk_ref, wv_ref, wg_ref, bg_ref,
                    q_ref, k_ref, v_ref, xc_ref, g_ref, ext):
    i = pl.program_id(0)
    n = pl.num_programs(0)
    xm = xm_ref[...]
    ext[0:HALO, :] = jnp.where(i > 0, prev_ref[...].astype(F32), 0.0)
    ext[HALO:HALO + TM, :] = xm.astype(F32)
    ext[HALO + TM:, :] = jnp.where(i < n - 1, next_ref[...].astype(F32), 0.0)
    conv = cb_ref[...]
    for j in range(CONV_W):
        conv = conv + cw_ref[j:j + 1, :] * ext[pl.ds(HALO - CONV_W // 2 + j, TM), :]
    xc = _silu(conv)
    xcb = xc.astype(BF16)
    xc_ref[...] = xcb

    for h in range(ML_H):
        cin = slice(h * ML_DV, (h + 1) * ML_DV)
        qk = _dot(xcb[:, cin], wqk_ref[h])
        q_ref[:, h * ML_DK:(h + 1) * ML_DK] = (qk[:, 0:ML_DK] * (ML_DK ** -0.5)).astype(BF16)
        k_ref[:, h * ML_DK:(h + 1) * ML_DK] = qk[:, ML_DK:].astype(BF16)
        v_ref[:, cin] = _dot(xm[:, cin], wv_ref[h]).astype(BF16)

    pre = _dot(xcb, wg_ref[...]) + bg_ref[...]
    lane = lax.broadcasted_iota(jnp.int32, pre.shape, 1)
    keep = _row_keep(pre.shape, TM)
    li = jnp.where(keep, pre, NEG_BIG)
    lf = jnp.where(keep, jnp.minimum(pre, 0.0) - jnp.log(1.0 + jnp.exp(-jnp.abs(pre))), 0.0)
    t = lax.broadcasted_iota(jnp.int32, (TM, TM), 0)
    s = lax.broadcasted_iota(jnp.int32, (TM, TM), 1)
    same = (t // ML_CHUNK) == (s // ML_CHUNK)
    tri_f = jnp.where(same & (s <= t), 1.0, 0.0).astype(BF16)
    tri_b = jnp.where(same & (s >= t), 1.0, 0.0).astype(BF16)
    lf_hi = lf.astype(BF16)
    lf_lo = (lf - lf_hi.astype(F32)).astype(BF16)
    cum_f = _dot(tri_f, lf_hi) + _dot(tri_f, lf_lo)
    cum_b = _dot(tri_b, lf_hi) + _dot(tri_b, lf_lo)
    g_ref[...] = jnp.where(lane < 2 * ML_H, li, jnp.where(lane < 3 * ML_H, cum_f, cum_b))


def _odd_mid(i, xm, cw, cb, wqk, wv, wg, bg):
    lp = xm.shape[0]
    row = lambda w: pl.BlockSpec((TM, w), lambda i: (i, 0))
    _full = _slab(i)
    per = TM // HALO
    last = lp // HALO - 1
    return pl.pallas_call(
        _odd_mid_kernel,
        grid=(lp // TM,),
        in_specs=[row(ML_W),
                  pl.BlockSpec((HALO, ML_W), lambda i: (jnp.maximum(i * per - 1, 0), 0)),
                  pl.BlockSpec((HALO, ML_W), lambda i: (jnp.minimum((i + 1) * per, last), 0)),
                  _full((CONV_W, ML_W)), _full((1, ML_W)),
                  _full((ML_H, ML_DV, 2 * ML_DK)), _full((ML_H, ML_DV, ML_DV)),
                  _full((ML_W, GATE_LANES)), _full((1, GATE_LANES))],
        out_specs=[row(ML_H * ML_DK), row(ML_H * ML_DK), row(ML_W), row(ML_W), row(GATE_LANES)],
        out_shape=[jax.ShapeDtypeStruct((lp, ML_H * ML_DK), BF16), jax.ShapeDtypeStruct((lp, ML_H * ML_DK), BF16),
                   jax.ShapeDtypeStruct((lp, ML_W), BF16), jax.ShapeDtypeStruct((lp, ML_W), BF16),
                   jax.ShapeDtypeStruct((lp, GATE_LANES), F32)],
        scratch_shapes=[pltpu.VMEM((TM + 2 * HALO, ML_W), F32)],
        compiler_params=_cparams("parallel"),
        name="odd_mid",
    )(xm, xm, xm, cw, cb, wqk, wv, wg, bg)


def _mlstm_chunk(q, k, v, g, gt, h, backward, c_sc, m_sc, slot):
    c = ML_CHUNK
    off = ML_H if backward else 0
    li_row = gt[off + h:off + h + 1, :]
    b_row = gt[2 * ML_H + off + h:2 * ML_H + off + h + 1, :]
    a_row = li_row - b_row
    a_col = g[:, off + h:off + h + 1] - g[:, 2 * ML_H + off + h:2 * ML_H + off + h + 1]
    b_last = jnp.broadcast_to(b_row[:, 0:1] if backward else b_row[:, c - 1:c], (1, c))
    m_prev = m_sc[slot]
    state = c_sc[slot]

    kq = _dot_nt(k, q)
    vt_aug = jnp.concatenate([v.T, jnp.ones((ML_AUG, c), BF16)], axis=0)
    qt = q.T.astype(F32)
    m_new = b_last + jnp.maximum(m_prev, jnp.max(a_row, axis=-1, keepdims=True))
    decay = jnp.exp(b_last + m_prev - m_new)
    w_row = jnp.exp(b_last + a_row - m_new)
    update = _dot((vt_aug.astype(F32) * w_row).astype(BF16), k)
    yield None

    s = lax.broadcasted_iota(jnp.int32, (c, c), 0)
    t = lax.broadcasted_iota(jnp.int32, (c, c), 1)
    allowed = (s >= t) if backward else (s <= t)
    dt = jnp.where(allowed, b_row + a_col, NEG)
    inter_log = b_row + m_prev
    m_t = jnp.maximum(jnp.max(dt, axis=0, keepdims=True), inter_log)
    swt = jnp.exp(dt - m_t) * kq
    inter = jnp.exp(inter_log - m_t)
    num = _dot(jnp.concatenate([vt_aug, state.astype(BF16)], axis=1),
               jnp.concatenate([swt.astype(BF16), (qt * inter).astype(BF16)], axis=0))
    yield None

    den = num[ML_DV:ML_DV + 1, :]
    out_t = num[0:ML_DV, :] / jnp.maximum(jnp.abs(den), jnp.exp(-m_t))
    c_sc[slot] = decay * state + update
    m_sc[slot] = m_new
    yield out_t.T


def _mlstm_kernel(qf_ref, kf_ref, vf_ref, gf_ref, qb_ref, kb_ref, vb_ref, gb_ref, hf_ref, hb_ref, c_sc, m_sc):
    @pl.when(pl.program_id(0) == 0)
    def _():
        c_sc[...] = jnp.zeros(c_sc.shape, F32)
        m_sc[...] = jnp.zeros(m_sc.shape, F32)

    chunks = []
    for backward, (q_ref, k_ref, v_ref, g_ref, h_ref) in enumerate(
            ((qf_ref, kf_ref, vf_ref, gf_ref, hf_ref), (qb_ref, kb_ref, vb_ref, gb_ref, hb_ref))):
        g = g_ref[...]
        gt = g.T
        for h in range(ML_H):
            gen = _mlstm_chunk(q_ref[:, h * ML_DK:(h + 1) * ML_DK], k_ref[:, h * ML_DK:(h + 1) * ML_DK],
                               v_ref[:, h * ML_DV:(h + 1) * ML_DV], g, gt, h, bool(backward),
                               c_sc, m_sc, backward * ML_H + h)
            chunks.append((gen, h_ref, h))
    for gen, _, _ in chunks:
        next(gen)
    for gen, _, _ in chunks:
        next(gen)
    for gen, h_ref, h in chunks:
        h_ref[:, h * ML_DV:(h + 1) * ML_DV] = next(gen).astype(BF16)


def _mlstm(q, k, v, gates):
    lp = q.shape[0]
    nch = lp // ML_CHUNK
    fwd = lambda w: pl.BlockSpec((ML_CHUNK, w), lambda i: (i, 0))
    bwd = lambda w: pl.BlockSpec((ML_CHUNK, w), lambda i: (nch - 1 - i, 0))
    qk_w = ML_H * ML_DK
    return pl.pallas_call(
        _mlstm_kernel,
        grid=(nch,),
        in_specs=[fwd(qk_w), fwd(qk_w), fwd(ML_W), fwd(GATE_LANES),
                  bwd(qk_w), bwd(qk_w), bwd(ML_W), bwd(GATE_LANES)],
        out_specs=[fwd(ML_W), bwd(ML_W)],
        out_shape=[jax.ShapeDtypeStruct((lp, ML_W), BF16)] * 2,
        scratch_shapes=[pltpu.VMEM((2 * ML_H, ML_DV + ML_AUG, ML_DK), F32),
                        pltpu.VMEM((2 * ML_H, 1, ML_CHUNK), F32)],
        compiler_params=_cparams("arbitrary"),
        name="mlstm",
    )(q, k, v, gates, q, k, v, gates)


def _odd_out_kernel(hf_ref, hb_ref, xc_ref, z_ref, h_ref, wo_ref, bo_ref, hn_ref, sk_ref, wout_ref, gpost_ref,
                    hout_ref):
    xcb = xc_ref[...]
    xc = xcb.astype(F32)
    og = _sigmoid(_dot(xcb, wo_ref[...]) + bo_ref[...])
    cell = hf_ref[...].astype(F32) + hb_ref[...].astype(F32)
    parts = []
    for h in range(ML_H):
        sl = slice(h * ML_DV, (h + 1) * ML_DV)
        parts.append(_rms(cell[:, sl], hn_ref[:, sl]))
    cell = jnp.concatenate(parts, axis=1)
    hh = (og * cell + sk_ref[...] * xc) * _silu(z_ref[...].astype(F32))
    out = _dot(hh.astype(BF16), wout_ref[...])
    hnew = h_ref[...] + _rms(out, gpost_ref[...])
    hout_ref[...] = jnp.where(_row_keep(hnew.shape, TM), hnew, 0.0)


def _odd_out(layer, i, hf, hb, xc, z, hres, wo, bo, hn, sk, wout, gpost):
    lp = hres.shape[0]
    row = lambda w: pl.BlockSpec((TM, w), lambda i: (i, 0))
    _full, _full_l = _slab(i), _slab(layer)
    return pl.pallas_call(
        _odd_out_kernel,
        grid=(lp // TM,),
        in_specs=[row(ML_W), row(ML_W), row(ML_W), row(ML_W), row(D_MODEL),
                  _full((ML_W, ML_W)), _full((1, ML_W)), _full((1, ML_W)), _full((1, ML_W)),
                  _full((ML_W, D_MODEL)), _full_l((1, D_MODEL))],
        out_specs=row(D_MODEL),
        out_shape=jax.ShapeDtypeStruct((lp, D_MODEL), F32),
        compiler_params=_cparams("parallel"),
        name="odd_out",
    )(hf, hb, xc, z, hres, wo, bo, hn, sk, wout, gpost)


def _rope_tables(n_real):
    rows = n_real // GRID_W
    freqs = ROPE_THETA ** (-jnp.arange(0, ROPE_AXIS, 2, dtype=F32) / ROPE_AXIS)
    half = freqs.shape[0]
    ang_r = jnp.arange(rows, dtype=F32)[:, None] * freqs[None]
    ang_c = jnp.arange(GRID_W, dtype=F32)[:, None] * freqs[None]
    per_row = lambda t: jnp.broadcast_to(t[:, None, :], (rows, GRID_W, half)).reshape(n_real, half)
    per_col = lambda t: jnp.broadcast_to(t[None, :, :], (rows, GRID_W, half)).reshape(n_real, half)
    cr, sr = per_row(jnp.cos(ang_r)), per_row(jnp.sin(ang_r))
    cc, sc = per_col(jnp.cos(ang_c)), per_col(jnp.sin(ang_c))
    cos = jnp.concatenate([cr, cr, cc, cc], axis=1)
    sin = jnp.concatenate([-sr, sr, -sc, sc], axis=1)
    cos = jnp.concatenate([jnp.ones((FRONT, HEAD_DIM), F32), cos], axis=0)
    sin = jnp.concatenate([jnp.zeros((FRONT, HEAD_DIM), F32), sin], axis=0)
    return cos, sin


def kernel(x, meta_tokens, norm_pre, norm_post, w_in_even, q_norm, k_norm, ssm_a_re, ssm_a_im, ssm_log_dt, ssm_b_re, ssm_b_im, ssm_c_re, ssm_c_im, ssm_d, w_glu, b_glu, w_out_even, w_in_odd, conv_w, conv_b, w_q, w_k, w_v, w_igate, b_igate, w_fgate, b_fgate, w_ogate, b_ogate, head_norm, mlstm_skip, w_out_odd):
    bsz, n_real, _ = x.shape
    assert bsz == 1 and n_real % TK == 0 and (n_real + FRONT) % TM == 0
    lp = n_real + FRONT
    depth = norm_pre.shape[0]
    hres = jnp.concatenate([jnp.zeros((N_DUMMY, D_MODEL), F32), meta_tokens.astype(F32), x[0]], axis=0)
    cos, sin = _rope_tables(n_real)
    n_chunks_pad = -(-(lp // S5_T) // 16) * 16

    bf = lambda a: a.astype(BF16)
    vec = lambda a: a.reshape(a.shape[0], 1, -1)
    norm_pre, norm_post = vec(norm_pre), vec(norm_post)
    w_in_even = bf(w_in_even)
    wvt = jnp.swapaxes(w_in_even[:, :, ATTN_W + KV_W:ATTN_W + 2 * KV_W], 1, 2)
    qg = vec(q_norm) * (HEAD_DIM ** -0.5 * math.log2(math.e))
    kmat, bmat, cmat, a_pow = jax.vmap(_s5_tables)(ssm_a_re, ssm_a_im, ssm_log_dt, ssm_b_re, ssm_b_im,
                                                   ssm_c_re, ssm_c_im, ssm_d)
    w_glu, w_out_even = bf(w_glu), bf(w_out_even)
    w_in_odd, w_ogate, w_out_odd, w_v = bf(w_in_odd), bf(w_ogate), bf(w_out_odd), bf(w_v)
    wqk = bf(jnp.concatenate([w_q, w_k], axis=3))
    gate_pad = GATE_LANES - 4 * ML_H
    wg = jnp.concatenate([w_igate[:, 0], w_igate[:, 1], w_fgate[:, 0], w_fgate[:, 1]], axis=2)
    wg = bf(jnp.pad(wg, ((0, 0), (0, 0), (0, gate_pad))))
    bg = jnp.concatenate([b_igate[:, 0], b_igate[:, 1], b_fgate[:, 0], b_fgate[:, 1]], axis=1)
    bg = vec(jnp.pad(bg, ((0, 0), (0, gate_pad))))

    for layer in range(depth):
        i = layer // 2
        if layer % 2 == 0:
            q, k, vt, ga, u, gb = _even_in(layer, i, hres, norm_pre, w_in_even, wvt, qg, vec(k_norm), cos, sin)
            logit_bound = (math.sqrt(HEAD_DIM) * math.log2(math.e)
                           * jnp.max(jnp.abs(q_norm[i])) * jnp.max(jnp.abs(k_norm[i])))
            o = lax.cond(logit_bound <= LOGIT_SAFE,
                         lambda q, k, vt: _attention(q, k, vt, False),
                         lambda q, k, vt: _attention(q, k, vt, True), q, k, vt)
            y = _s5_unpack(_s5(i, _s5_pack(u, n_chunks_pad), kmat, bmat, cmat, a_pow), lp)
            hres = _even_out(layer, i, o, ga, y, gb, hres, w_glu, vec(b_glu), w_out_even, norm_post)
        else:
            xm, z = _odd_in(layer, i, hres, norm_pre, w_in_odd)
            qm, km, vm, xc, gates = _odd_mid(i, xm, conv_w, vec(conv_b), wqk, w_v, wg, bg)
            hf, hb = _mlstm(qm, km, vm, gates)
            hres = _odd_out(layer, i, hf, hb, xc, z, hres, w_ogate, vec(b_ogate), vec(head_norm), vec(mlstm_skip),
                            w_out_odd, norm_post)
    return hres[FRONT:].reshape(1, n_real, D_MODEL)
```

```python
import functools
import math

import jax
import jax.numpy as jnp
from jax import lax
from jax.experimental import pallas as pl
from jax.experimental.pallas import tpu as pltpu

F32 = jnp.float32
BF16 = jnp.bfloat16

D_MODEL = 1024
N_META = 16
FRONT = 128
N_DUMMY = FRONT - N_META
GRID_W = 64
EPS = 1e-6
NEG = -1e30

HEADS = 8
KV_HEADS = 2
KV_REP = HEADS // KV_HEADS
HEAD_DIM = 128
ROPE_AXIS = HEAD_DIM // 2
ROPE_THETA = 10000.0
ATTN_W = HEADS * HEAD_DIM
KV_W = KV_HEADS * HEAD_DIM
TQ = 384
TQS = 128
TK = 4096
VT_ROWS = HEAD_DIM + 16
LOGIT_SAFE = 60.0

S5_W = 1024
S5_P = 16
S5_G = S5_W // S5_P
S5_N = 64
S5_T = 16
S5_GB = 8
S5_PB = S5_GB // 2
S5_CW = S5_T * S5_P
S5_UNROLL = 8

ML_W = 2048
ML_H = 8
ML_DV = 256
ML_DK = 128
CONV_W = 5
ML_CHUNK = 128
ML_AUG = 16
NEG_BIG = -1e9

TM = 384
VMEM_LIMIT = 56 * 1024 * 1024


def _cparams(*sem):
    return pltpu.CompilerParams(dimension_semantics=sem, vmem_limit_bytes=VMEM_LIMIT)


def _rms(x, g):
    return x * lax.rsqrt(jnp.mean(x * x, axis=-1, keepdims=True) + EPS) * g


def _sigmoid(x):
    return 0.5 * jnp.tanh(0.5 * x) + 0.5


def _silu(x):
    return x * _sigmoid(x)


def _gelu_tanh(x):
    return 0.5 * x * (1.0 + jnp.tanh(math.sqrt(2.0 / math.pi) * (x + 0.044715 * (x * x * x))))


def _dot(a, b):
    return jnp.dot(a, b, preferred_element_type=F32)


def _dot_nt(a, b):
    return lax.dot_general(a, b, (((1,), (1,)), ((), ())), preferred_element_type=F32)


def _dot_tn(a, b):
    return lax.dot_general(a, b, (((0,), (0,)), ((), ())), preferred_element_type=F32)


def _slab(layer):
    def spec(shape):
        return pl.BlockSpec((None,) + tuple(shape), lambda *_: (layer,) + (0,) * len(shape))
    return spec


def _even_in_kernel(h_ref, gpre_ref, w_ref, wvt_ref, qg_ref, kg_ref, cos_ref, sin_ref,
                    q_ref, k_ref, vt_ref, ga_ref, u_ref, gb_ref):
    hn = _rms(h_ref[...], gpre_ref[...]).astype(BF16)
    cos = cos_ref[...]
    sin = sin_ref[...]
    lane = lax.broadcasted_iota(jnp.int32, cos.shape, 1)
    first_half = (lane % ROPE_AXIS) < (ROPE_AXIS // 2)

    def head(x, g):
        y = _rms(x, g)
        partner = jnp.where(first_half, pltpu.roll(y, HEAD_DIM - ROPE_AXIS // 2, 1),
                            pltpu.roll(y, ROPE_AXIS // 2, 1))
        return (y * cos + partner * sin).astype(BF16)

    qsec = _dot(hn, w_ref[:, 0:ATTN_W])
    for h in range(HEADS):
        sl = slice(h * HEAD_DIM, (h + 1) * HEAD_DIM)
        q_ref[:, sl] = head(qsec[:, sl], qg_ref[...])
    ksec = _dot(hn, w_ref[:, ATTN_W:ATTN_W + KV_W])
    for h in range(KV_HEADS):
        sl = slice(h * HEAD_DIM, (h + 1) * HEAD_DIM)
        k_ref[:, sl] = head(ksec[:, sl], kg_ref[...])
    vt = _dot_nt(wvt_ref[...], hn).astype(BF16)
    for g in range(KV_HEADS):
        vt_ref[g * VT_ROWS:g * VT_ROWS + HEAD_DIM, :] = vt[g * HEAD_DIM:(g + 1) * HEAD_DIM]
        vt_ref[g * VT_ROWS + HEAD_DIM:(g + 1) * VT_ROWS, :] = jnp.ones((VT_ROWS - HEAD_DIM, TM), BF16)
    c0 = ATTN_W + 2 * KV_W
    ga_ref[...] = _dot(hn, w_ref[:, c0:c0 + ATTN_W]).astype(BF16)
    c0 += ATTN_W
    u_ref[...] = _dot(hn, w_ref[:, c0:c0 + S5_W])
    c0 += S5_W
    gb_ref[...] = _dot(hn, w_ref[:, c0:c0 + S5_W]).astype(BF16)


def _even_in(layer, i, hres, gpre, w_in, wvt, qg, kg, cos, sin):
    lp = hres.shape[0]
    win_w = w_in.shape[-1]
    _full, _full_l = _slab(i), _slab(layer)
    row = lambda w: pl.BlockSpec((TM, w), lambda i: (i, 0))
    vt_rows = KV_HEADS * VT_ROWS
    col = pl.BlockSpec((vt_rows, TM), lambda i: (0, i))
    widths = [ATTN_W, KV_W, None, ATTN_W, S5_W, S5_W]
    return pl.pallas_call(
        _even_in_kernel,
        grid=(lp // TM,),
        in_specs=[row(D_MODEL), _full_l((1, D_MODEL)), _full((D_MODEL, win_w)), _full((KV_W, D_MODEL)),
                  _full((1, HEAD_DIM)), _full((1, HEAD_DIM)), row(HEAD_DIM), row(HEAD_DIM)],
        out_specs=[col if w is None else row(w) for w in widths],
        out_shape=[jax.ShapeDtypeStruct((vt_rows, lp) if w is None else (lp, w), F32 if n == 4 else BF16)
                   for n, w in enumerate(widths)],
        compiler_params=_cparams("parallel"),
        name="even_in",
    )(hres, gpre, w_in, wvt, qg, kg, cos, sin)


def _attn_kernel(q_ref, k_ref, vt_ref, o_ref, m_sc, acc_sc, p_sc, *, n_kv_tiles, stabilised):
    qs = [jnp.concatenate([q_ref[a * TQS:(a + 1) * TQS, r * HEAD_DIM:(r + 1) * HEAD_DIM] for r in range(KV_REP)],
                          axis=0) for a in range(TQ // TQS)]

    def scores(kb, a, mask_dummy=False):
        st = _dot_nt(kb, qs[a])
        if mask_dummy:
            key = lax.broadcasted_iota(jnp.int32, st.shape, 0)
            st = jnp.where(key >= N_DUMMY, st, NEG)
        return st

    def tile_start(j):
        return pl.multiple_of(FRONT + j * TK, 128)

    k0, vt0 = k_ref[0:FRONT, :], vt_ref[:, 0:FRONT]
    if stabilised:
        def update(kb, vtb, mask_dummy):
            for a in range(len(qs)):
                st = scores(kb, a, mask_dummy)
                m_old = m_sc[a]
                m_new = jnp.maximum(m_old, jnp.max(st, axis=0, keepdims=True))
                p = jnp.exp2(st - m_new).astype(BF16)
                acc_sc[a] = jnp.exp2(m_old - m_new) * acc_sc[a] + _dot(vtb, p)
                m_sc[a] = m_new

        m_sc[...] = jnp.full(m_sc.shape, NEG, F32)
        acc_sc[...] = jnp.zeros(acc_sc.shape, F32)
        update(k0, vt0, True)

        def body(j, carry):
            update(k_ref[pl.ds(tile_start(j), TK), :], vt_ref[:, pl.ds(tile_start(j), TK)], False)
            return carry

        lax.fori_loop(0, n_kv_tiles, body, 0)
    else:
        for a in range(len(qs)):
            acc_sc[a] = _dot(vt0, jnp.exp2(scores(k0, a, True).astype(BF16)))
            p_sc[0, a] = jnp.exp2(scores(k_ref[FRONT:FRONT + TK, :], a).astype(BF16))

        def body(j, carry):
            cur, prev = j % 2, (j + 1) % 2
            kb = k_ref[pl.ds(tile_start(j), TK), :]
            vtb = vt_ref[:, pl.ds(tile_start(j - 1), TK)]
            for a in range(len(qs)):
                acc_sc[a] += _dot(vtb, p_sc[prev, a])
            for a in range(len(qs)):
                p_sc[cur, a] = jnp.exp2(scores(kb, a).astype(BF16))
            return carry

        lax.fori_loop(1, n_kv_tiles, body, 0)
        last = (n_kv_tiles - 1) % 2
        vtb = vt_ref[:, FRONT + (n_kv_tiles - 1) * TK:FRONT + n_kv_tiles * TK]
        for a in range(len(qs)):
            acc_sc[a] += _dot(vtb, p_sc[last, a])

    for a in range(TQ // TQS):
        ot = acc_sc[a, 0:HEAD_DIM, :] / acc_sc[a, HEAD_DIM:HEAD_DIM + 1, :]
        for r in range(KV_REP):
            o_ref[a * TQS:(a + 1) * TQS, r * HEAD_DIM:(r + 1) * HEAD_DIM] = (
                ot[:, r * TQS:(r + 1) * TQS].T.astype(BF16))


def _attention(q, k, vt, stabilised):
    lp = q.shape[0]
    n_kv_tiles = (lp - FRONT) // TK
    gw = KV_REP * HEAD_DIM
    n_sub = TQ // TQS
    return pl.pallas_call(
        functools.partial(_attn_kernel, n_kv_tiles=n_kv_tiles, stabilised=stabilised),
        grid=(KV_HEADS, lp // TQ),
        in_specs=[pl.BlockSpec((TQ, gw), lambda g, i: (i, g)),
                  pl.BlockSpec((lp, HEAD_DIM), lambda g, i: (0, g)),
                  pl.BlockSpec((VT_ROWS, lp), lambda g, i: (g, 0))],
        out_specs=pl.BlockSpec((TQ, gw), lambda g, i: (i, g)),
        out_shape=jax.ShapeDtypeStruct((lp, ATTN_W), BF16),
        scratch_shapes=[pltpu.VMEM((n_sub, 1, KV_REP * TQS), F32),
                        pltpu.VMEM((n_sub, VT_ROWS, KV_REP * TQS), F32),
                        pltpu.VMEM((2, n_sub, TK, KV_REP * TQS), BF16)],
        compiler_params=_cparams("parallel", "parallel"),
        name="attention_stabilised" if stabilised else "attention",
    )(q, k, vt)


def _s5_tables(a_re, a_im, log_dt, b_re, b_im, c_re, c_im, d_skip):
    t = S5_T
    lag = jnp.arange(t + 1, dtype=F32)
    per_dir = []
    for d in range(2):
        a = lax.complex(a_re[d], a_im[d])
        adt = a * jnp.exp(log_dt[d])[:, None]
        a_bar = jnp.exp(adt)
        b_bar = ((a_bar - 1.0) / a)[..., None] * lax.complex(b_re[d], b_im[d])
        c = lax.complex(c_re[d], c_im[d])
        pw = jnp.exp(adt[None] * lag[:, None, None].astype(jnp.complex64))
        klag = jnp.real(jnp.einsum('gon,dgn,gni->dgoi', c, pw[:t], b_bar,
                                   precision=lax.Precision.HIGHEST))
        per_dir.append((b_bar, c, pw, klag))
    (bb0, c0, pw0, kl0), (bb1, c1, pw1, kl1) = per_dir
    ii = jnp.arange(t)[:, None, None]
    jj = jnp.arange(t)[None, :, None]
    dd = jnp.arange(t)[None, None, :]
    place_f = (jj - ii == dd).astype(F32)
    place_b = (ii - jj == dd).astype(F32)
    skip = d_skip.reshape(S5_G, S5_P)[:, :, None] * jnp.eye(S5_P, dtype=F32)
    kl0 = kl0.at[0].add(skip)
    kmat = (jnp.einsum('ijd,dgop->gipjo', place_f, kl0, precision=lax.Precision.HIGHEST)
            + jnp.einsum('ijd,dgop->gipjo', place_b, kl1, precision=lax.Precision.HIGHEST)
            ).reshape(S5_G, S5_CW, S5_CW)

    bf = pw0[:t][::-1][..., None] * bb0[None]
    bb = pw1[:t][..., None] * bb1[None]
    cf = c0[None] * pw0[1:t + 1][:, :, None, :]
    cb = c1[None] * pw1[1:t + 1][::-1][:, :, None, :]

    def in_mat(x):
        return x.astype(BF16).transpose(1, 0, 3, 2).reshape(S5_G, S5_CW, S5_N)

    def out_mat(x):
        return x.astype(BF16).transpose(1, 3, 0, 2).reshape(S5_G, S5_N, S5_CW)

    n_pair = S5_G // 2
    zero_in = jnp.zeros((n_pair, S5_CW, S5_N), BF16)
    zero_out = jnp.zeros((n_pair, S5_N, S5_CW), BF16)

    def pair_in(x):
        e, o = x[0::2], x[1::2]
        return jnp.concatenate([jnp.concatenate([e, zero_in], axis=2),
                                jnp.concatenate([zero_in, o], axis=2)], axis=1)

    def pair_out(x):
        e, o = x[0::2], x[1::2]
        return jnp.concatenate([jnp.concatenate([e, zero_out], axis=2),
                                jnp.concatenate([zero_out, o], axis=2)], axis=1)

    ins = [in_mat(jnp.real(bf)), in_mat(jnp.imag(bf)), in_mat(jnp.real(bb)), in_mat(jnp.imag(bb))]
    outs = [out_mat(jnp.real(cf)), out_mat(-jnp.imag(cf)), out_mat(jnp.real(cb)), out_mat(-jnp.imag(cb))]
    ins, outs = lax.optimization_barrier((ins, outs))
    bmat = jnp.concatenate([pair_in(m) for m in ins], axis=2)
    cmat = jnp.concatenate([pair_out(m) for m in outs], axis=1)

    def pair_row(x):
        return x.reshape(S5_G // S5_GB, 1, S5_PB * 2 * S5_N)

    a_pow = jnp.concatenate([pair_row(jnp.real(pw0[t])), pair_row(jnp.imag(pw0[t])),
                             pair_row(jnp.real(pw1[t])), pair_row(jnp.imag(pw1[t]))], axis=1)
    return kmat.astype(BF16), bmat, cmat, a_pow


LANE_GROUPS = 128 // S5_P


def _swap_token_group(xs):
    block = lax.broadcasted_iota(jnp.int32, xs[0].shape, 1) // S5_P
    xs = list(xs)
    d = LANE_GROUPS // 2
    while d:
        upper = (block & d) != 0
        for a in range(LANE_GROUPS):
            if a & d:
                continue
            lo, hi = xs[a], xs[a + d]
            xs[a] = jnp.where(upper, pltpu.roll(hi, S5_P * d, 1), lo)
            xs[a + d] = jnp.where(upper, hi, pltpu.roll(lo, 128 - S5_P * d, 1))
        d //= 2
    return xs


def _s5_pack_kernel(u_ref, o_ref, *, n_chunks):
    n_main = n_chunks // 16 * 16
    for half in range(S5_T // LANE_GROUPS):
        xs = [u_ref[pl.ds(LANE_GROUPS * half + i, n_chunks, stride=S5_T), :] for i in range(LANE_GROUPS)]
        for g, y in enumerate(_swap_token_group(xs)):
            cols = slice(g * S5_CW + half * 128, g * S5_CW + (half + 1) * 128)
            o_ref[0:n_main, cols] = y[0:n_main].astype(BF16)
            if o_ref.shape[0] > n_main:
                tail = jnp.concatenate([y[n_main:], jnp.zeros((o_ref.shape[0] - n_chunks, 128), F32)], axis=0)
                o_ref[n_main:, cols] = tail.astype(BF16)


def _s5_unpack_kernel(y_ref, o_ref, *, n_chunks):
    for half in range(S5_T // LANE_GROUPS):
        ys = [y_ref[:, g * S5_CW + half * 128:g * S5_CW + (half + 1) * 128].astype(F32)[0:n_chunks]
              for g in range(LANE_GROUPS)]
        for i, x in enumerate(_swap_token_group(ys)):
            o_ref[pl.ds(LANE_GROUPS * half + i, n_chunks, stride=S5_T), :] = x


def _s5_pack(u, n_chunks_pad):
    lp = u.shape[0]
    bw = LANE_GROUPS * S5_CW
    return pl.pallas_call(
        functools.partial(_s5_pack_kernel, n_chunks=lp // S5_T),
        grid=(S5_W // 128,),
        in_specs=[pl.BlockSpec((lp, 128), lambda i: (0, i))],
        out_specs=pl.BlockSpec((n_chunks_pad, bw), lambda i: (0, i)),
        out_shape=jax.ShapeDtypeStruct((n_chunks_pad, S5_G * S5_CW), BF16),
        compiler_params=_cparams("parallel"),
        name="s5_pack",
    )(u)


def _s5_unpack(yc, lp):
    bw = LANE_GROUPS * S5_CW
    return pl.pallas_call(
        functools.partial(_s5_unpack_kernel, n_chunks=lp // S5_T),
        grid=(S5_W // 128,),
        in_specs=[pl.BlockSpec((yc.shape[0], bw), lambda i: (0, i))],
        out_specs=pl.BlockSpec((lp, 128), lambda i: (0, i)),
        out_shape=jax.ShapeDtypeStruct((lp, S5_W), F32),
        compiler_params=_cparams("parallel"),
        name="s5_unpack",
    )(yc)


def _s5_kernel(u_ref, k_ref, b_ref, c_ref, a_ref, y_ref, fre, fim, bre, bim, *, n_chunks):
    pw = 2 * S5_N
    for p in range(S5_PB):
        s = _dot(u_ref[:, p * 2 * S5_CW:(p + 1) * 2 * S5_CW], b_ref[p])
        cols = slice(p * pw, (p + 1) * pw)
        fre[:, cols] = s[:, 0:pw]
        fim[:, cols] = s[:, pw:2 * pw]
        bre[:, cols] = s[:, 2 * pw:3 * pw]
        bim[:, cols] = s[:, 3 * pw:4 * pw]

    far, fai, bar, bai = a_ref[0, 0:1, :], a_ref[0, 1:2, :], a_ref[0, 2:3, :], a_ref[0, 3:4, :]

    def step(i, carry):
        xr, xi, yr, yi = carry
        rf = pl.ds(i, 1)
        rb = pl.ds(n_chunks - 1 - i, 1)
        sr, si = fre[rf, :], fim[rf, :]
        tr, ti = bre[rb, :], bim[rb, :]
        fre[rf, :] = xr
        fim[rf, :] = xi
        bre[rb, :] = yr
        bim[rb, :] = yi
        return (far * xr - fai * xi + sr, far * xi + fai * xr + si,
                bar * yr - bai * yi + tr, bar * yi + bai * yr + ti)

    zero = jnp.zeros((1, S5_PB * pw), F32)
    lax.fori_loop(0, n_chunks, step, (zero, zero, zero, zero), unroll=S5_UNROLL)

    for p in range(S5_PB):
        cols = slice(p * pw, (p + 1) * pw)
        state = jnp.concatenate([fre[:, cols], fim[:, cols], bre[:, cols], bim[:, cols]], axis=1).astype(BF16)
        carry_out = _dot(state, c_ref[p])
        for e in range(2):
            g = 2 * p + e
            gc = slice(g * S5_CW, (g + 1) * S5_CW)
            y = _dot(u_ref[:, gc], k_ref[g]) + carry_out[:, e * S5_CW:(e + 1) * S5_CW]
            y_ref[:, gc] = y.astype(BF16)


def _s5(layer, u_chunks, kmat, bmat, cmat, a_pow):
    n_chunks = u_chunks.shape[0]
    bw = S5_GB * S5_CW
    sw = S5_PB * 2 * S5_N
    return pl.pallas_call(
        functools.partial(_s5_kernel, n_chunks=n_chunks),
        grid=(S5_G // S5_GB,),
        in_specs=[pl.BlockSpec((n_chunks, bw), lambda i: (0, i)),
                  pl.BlockSpec((None, S5_GB, S5_CW, S5_CW), lambda i: (layer, i, 0, 0)),
                  pl.BlockSpec((None, S5_PB, 2 * S5_CW, 8 * S5_N), lambda i: (layer, i, 0, 0)),
                  pl.BlockSpec((None, S5_PB, 8 * S5_N, 2 * S5_CW), lambda i: (layer, i, 0, 0)),
                  pl.BlockSpec((None, 1, 4, sw), lambda i: (layer, i, 0, 0))],
        out_specs=pl.BlockSpec((n_chunks, bw), lambda i: (0, i)),
        out_shape=jax.ShapeDtypeStruct(u_chunks.shape, BF16),
        scratch_shapes=[pltpu.VMEM((n_chunks, sw), F32)] * 4,
        compiler_params=_cparams("parallel"),
        name="s5_scan",
    )(u_chunks, kmat, bmat, cmat, a_pow)


def _row_keep(shape, tile_rows):
    row = pl.program_id(0) * tile_rows + lax.broadcasted_iota(jnp.int32, shape, 0)
    return row >= N_DUMMY


def _even_out_kernel(o_ref, ga_ref, y_ref, gb_ref, h_ref, wglu_ref, bglu_ref, wout_ref, gpost_ref, hout_ref):
    ya = o_ref[...].astype(F32) * _silu(ga_ref[...].astype(F32))
    yb = _gelu_tanh(y_ref[...].astype(F32))
    yb = yb * _sigmoid(_dot(yb.astype(BF16), wglu_ref[...]) + bglu_ref[...])
    yb = yb * _silu(gb_ref[...].astype(F32))
    out = _dot(ya.astype(BF16), wout_ref[0:ATTN_W, :]) + _dot(yb.astype(BF16), wout_ref[ATTN_W:, :])
    hnew = h_ref[...] + _rms(out, gpost_ref[...])
    hout_ref[...] = jnp.where(_row_keep(hnew.shape, TM), hnew, 0.0)


def _even_out(layer, i, o, ga, y, gb, hres, wglu, bglu, wout, gpost):
    lp = hres.shape[0]
    row = lambda w: pl.BlockSpec((TM, w), lambda i: (i, 0))
    _full, _full_l = _slab(i), _slab(layer)
    return pl.pallas_call(
        _even_out_kernel,
        grid=(lp // TM,),
        in_specs=[row(ATTN_W), row(ATTN_W), row(S5_W), row(S5_W), row(D_MODEL),
                  _full((S5_W, S5_W)), _full((1, S5_W)), _full((ATTN_W + S5_W, D_MODEL)), _full_l((1, D_MODEL))],
        out_specs=row(D_MODEL),
        out_shape=jax.ShapeDtypeStruct((lp, D_MODEL), F32),
        compiler_params=_cparams("parallel"),
        name="even_out",
    )(o, ga, y, gb, hres, wglu, bglu, wout, gpost)


def _odd_in_kernel(h_ref, gpre_ref, w_ref, xm_ref, z_ref):
    hn = _rms(h_ref[...], gpre_ref[...]).astype(BF16)
    xm_ref[...] = _dot(hn, w_ref[:, 0:ML_W]).astype(BF16)
    z_ref[...] = _dot(hn, w_ref[:, ML_W:]).astype(BF16)


def _odd_in(layer, i, hres, gpre, w_in):
    lp = hres.shape[0]
    row = lambda w: pl.BlockSpec((TM, w), lambda i: (i, 0))
    return pl.pallas_call(
        _odd_in_kernel,
        grid=(lp // TM,),
        in_specs=[row(D_MODEL), _slab(layer)((1, D_MODEL)), _slab(i)((D_MODEL, 2 * ML_W))],
        out_specs=[row(ML_W), row(ML_W)],
        out_shape=[jax.ShapeDtypeStruct((lp, ML_W), BF16)] * 2,
        compiler_params=_cparams("parallel"),
        name="odd_in",
    )(hres, gpre, w_in)


HALO = 16
CONV_BLOCK = 128
GATE_LANES = 128


def _conv_shift_matrix():
    offs = [j - CONV_W // 2 for j in range(CONV_W) if j != CONV_W // 2]
    r = jnp.arange(CONV_BLOCK)[None, :, None]
    c = jnp.arange(CONV_BLOCK + 2 * HALO)[None, None, :]
    off = jnp.asarray(offs)[:, None, None]
    return (c == r + HALO + off).astype(BF16).reshape(len(offs) * CONV_BLOCK, CONV_BLOCK + 2 * HALO)


def _odd_mid_kernel(xm_ref, prev_ref, next_ref, sh_ref, cw_ref, cb_ref, wqk_ref, wv_ref, wg_ref, bg_ref,
                    q_ref, k_ref, v_ref, xc_ref, g_ref, ext):
    i = pl.program_id(0)
    n = pl.num_programs(0)
    xm = xm_ref[...]
    ext[0:HALO, :] = jnp.where(i > 0, prev_ref[...], jnp.zeros((), BF16))
    ext[HALO:HALO + TM, :] = xm
    ext[HALO + TM:, :] = jnp.where(i < n - 1, next_ref[...], jnp.zeros((), BF16))
    centre = CONV_W // 2
    for b in range(TM // CONV_BLOCK):
        rows = slice(b * CONV_BLOCK, (b + 1) * CONV_BLOCK)
        shifted = _dot(sh_ref[...], ext[b * CONV_BLOCK:(b + 1) * CONV_BLOCK + 2 * HALO, :])
        conv = cb_ref[...] + cw_ref[centre:centre + 1, :] * xm[rows].astype(F32)
        for n_tap, j in enumerate(j for j in range(CONV_W) if j != centre):
            conv = conv + cw_ref[j:j + 1, :] * shifted[n_tap * CONV_BLOCK:(n_tap + 1) * CONV_BLOCK]
        xc_ref[rows, :] = _silu(conv).astype(BF16)
    xcb = xc_ref[...]

    for h in range(ML_H):
        cin = slice(h * ML_DV, (h + 1) * ML_DV)
        qk = _dot(xcb[:, cin], wqk_ref[h])
        q_ref[:, h * ML_DK:(h + 1) * ML_DK] = (qk[:, 0:ML_DK] * (ML_DK ** -0.5)).astype(BF16)
        k_ref[:, h * ML_DK:(h + 1) * ML_DK] = qk[:, ML_DK:].astype(BF16)
        v_ref[:, cin] = _dot(xm[:, cin], wv_ref[h]).astype(BF16)

    pre = _dot(xcb, wg_ref[...]) + bg_ref[...]
    lane = lax.broadcasted_iota(jnp.int32, pre.shape, 1)
    keep = _row_keep(pre.shape, TM)
    li = jnp.where(keep, pre, NEG_BIG)
    lf = jnp.where(keep, jnp.minimum(pre, 0.0) - jnp.log(1.0 + jnp.exp(-jnp.abs(pre))), 0.0)
    t = lax.broadcasted_iota(jnp.int32, (TM, TM), 0)
    s = lax.broadcasted_iota(jnp.int32, (TM, TM), 1)
    same = (t // ML_CHUNK) == (s // ML_CHUNK)
    tri_f = jnp.where(same & (s <= t), 1.0, 0.0).astype(BF16)
    tri_b = jnp.where(same & (s >= t), 1.0, 0.0).astype(BF16)
    lf_hi = lf.astype(BF16)
    lf_lo = (lf - lf_hi.astype(F32)).astype(BF16)
    cum_f = _dot(tri_f, lf_hi) + _dot(tri_f, lf_lo)
    cum_b = _dot(tri_b, lf_hi) + _dot(tri_b, lf_lo)
    g_ref[...] = jnp.where(lane < 2 * ML_H, li, jnp.where(lane < 3 * ML_H, cum_f, cum_b))


def _odd_mid(i, xm, cw, cb, wqk, wv, wg, bg):
    lp = xm.shape[0]
    row = lambda w: pl.BlockSpec((TM, w), lambda i: (i, 0))
    _full = _slab(i)
    per = TM // HALO
    last = lp // HALO - 1
    shift = _conv_shift_matrix()
    return pl.pallas_call(
        _odd_mid_kernel,
        grid=(lp // TM,),
        in_specs=[row(ML_W),
                  pl.BlockSpec((HALO, ML_W), lambda i: (jnp.maximum(i * per - 1, 0), 0)),
                  pl.BlockSpec((HALO, ML_W), lambda i: (jnp.minimum((i + 1) * per, last), 0)),
                  pl.BlockSpec(shift.shape, lambda i: (0, 0)),
                  _full((CONV_W, ML_W)), _full((1, ML_W)),
                  _full((ML_H, ML_DV, 2 * ML_DK)), _full((ML_H, ML_DV, ML_DV)),
                  _full((ML_W, GATE_LANES)), _full((1, GATE_LANES))],
        out_specs=[row(ML_H * ML_DK), row(ML_H * ML_DK), row(ML_W), row(ML_W), row(GATE_LANES)],
        out_shape=[jax.ShapeDtypeStruct((lp, ML_H * ML_DK), BF16), jax.ShapeDtypeStruct((lp, ML_H * ML_DK), BF16),
                   jax.ShapeDtypeStruct((lp, ML_W), BF16), jax.ShapeDtypeStruct((lp, ML_W), BF16),
                   jax.ShapeDtypeStruct((lp, GATE_LANES), F32)],
        scratch_shapes=[pltpu.VMEM((TM + 2 * HALO, ML_W), BF16)],
        compiler_params=_cparams("parallel"),
        name="odd_mid",
    )(xm, xm, xm, shift, cw, cb, wqk, wv, wg, bg)


def _mlstm_chunk(q, k, v, g, gt, h, backward, c_sc, m_sc, slot):
    c = ML_CHUNK
    off = ML_H if backward else 0
    li_row = gt[off + h:off + h + 1, :]
    b_row = gt[2 * ML_H + off + h:2 * ML_H + off + h + 1, :]
    a_row = li_row - b_row
    a_col = g[:, off + h:off + h + 1] - g[:, 2 * ML_H + off + h:2 * ML_H + off + h + 1]
    b_last = jnp.broadcast_to(b_row[:, 0:1] if backward else b_row[:, c - 1:c], (1, c))
    m_prev = m_sc[slot]
    state = c_sc[slot]

    kq = _dot_nt(k, q)
    vt_aug = jnp.concatenate([v.T, jnp.ones((ML_AUG, c), BF16)], axis=0)
    qt = q.T.astype(F32)
    m_new = b_last + jnp.maximum(m_prev, jnp.max(a_row, axis=-1, keepdims=True))
    decay = jnp.exp(b_last + m_prev - m_new)
    w_row = jnp.exp(b_last + a_row - m_new)
    update = _dot((vt_aug.astype(F32) * w_row).astype(BF16), k)
    yield None

    s = lax.broadcasted_iota(jnp.int32, (c, c), 0)
    t = lax.broadcasted_iota(jnp.int32, (c, c), 1)
    allowed = (s >= t) if backward else (s <= t)
    dt = jnp.where(allowed, b_row + a_col, NEG)
    inter_log = b_row + m_prev
    m_t = jnp.maximum(jnp.max(dt, axis=0, keepdims=True), inter_log)
    swt = jnp.exp(dt - m_t) * kq
    inter = jnp.exp(inter_log - m_t)
    num = _dot(jnp.concatenate([vt_aug, state.astype(BF16)], axis=1),
               jnp.concatenate([swt.astype(BF16), (qt * inter).astype(BF16)], axis=0))
    yield None

    den = num[ML_DV:ML_DV + 1, :]
    out_t = num[0:ML_DV, :] / jnp.maximum(jnp.abs(den), jnp.exp(-m_t))
    c_sc[slot] = decay * state + update
    m_sc[slot] = m_new
    yield out_t.T


def _mlstm_kernel(qf_ref, kf_ref, vf_ref, gf_ref, qb_ref, kb_ref, vb_ref, gb_ref, hf_ref, hb_ref, c_sc, m_sc):
    @pl.when(pl.program_id(0) == 0)
    def _():
        c_sc[...] = jnp.zeros(c_sc.shape, F32)
        m_sc[...] = jnp.zeros(m_sc.shape, F32)

    chunks = []
    for backward, (q_ref, k_ref, v_ref, g_ref, h_ref) in enumerate(
            ((qf_ref, kf_ref, vf_ref, gf_ref, hf_ref), (qb_ref, kb_ref, vb_ref, gb_ref, hb_ref))):
        g = g_ref[...]
        gt = g.T
        for h in range(ML_H):
            gen = _mlstm_chunk(q_ref[:, h * ML_DK:(h + 1) * ML_DK], k_ref[:, h * ML_DK:(h + 1) * ML_DK],
                               v_ref[:, h * ML_DV:(h + 1) * ML_DV], g, gt, h, bool(backward),
                               c_sc, m_sc, backward * ML_H + h)
            chunks.append((gen, h_ref, h))
    for gen, _, _ in chunks:
        next(gen)
    for gen, _, _ in chunks:
        next(gen)
    for gen, h_ref, h in chunks:
        h_ref[:, h * ML_DV:(h + 1) * ML_DV] = next(gen).astype(BF16)


def _mlstm(q, k, v, gates):
    lp = q.shape[0]
    nch = lp // ML_CHUNK
    fwd = lambda w: pl.BlockSpec((ML_CHUNK, w), lambda i: (i, 0))
    bwd = lambda w: pl.BlockSpec((ML_CHUNK, w), lambda i: (nch - 1 - i, 0))
    qk_w = ML_H * ML_DK
    return pl.pallas_call(
        _mlstm_kernel,
        grid=(nch,),
        in_specs=[fwd(qk_w), fwd(qk_w), fwd(ML_W), fwd(GATE_LANES),
                  bwd(qk_w), bwd(qk_w), bwd(ML_W), bwd(GATE_LANES)],
        out_specs=[fwd(ML_W), bwd(ML_W)],
        out_shape=[jax.ShapeDtypeStruct((lp, ML_W), BF16)] * 2,
        scratch_shapes=[pltpu.VMEM((2 * ML_H, ML_DV + ML_AUG, ML_DK), F32),
                        pltpu.VMEM((2 * ML_H, 1, ML_CHUNK), F32)],
        compiler_params=_cparams("arbitrary"),
        name="mlstm",
    )(q, k, v, gates, q, k, v, gates)


def _odd_out_kernel(hf_ref, hb_ref, xc_ref, z_ref, h_ref, wo_ref, bo_ref, hn_ref, sk_ref, wout_ref, gpost_ref,
                    hout_ref):
    xcb = xc_ref[...]
    xc = xcb.astype(F32)
    og = _sigmoid(_dot(xcb, wo_ref[...]) + bo_ref[...])
    cell = hf_ref[...].astype(F32) + hb_ref[...].astype(F32)
    parts = []
    for h in range(ML_H):
        sl = slice(h * ML_DV, (h + 1) * ML_DV)
        parts.append(_rms(cell[:, sl], hn_ref[:, sl]))
    cell = jnp.concatenate(parts, axis=1)
    hh = (og * cell + sk_ref[...] * xc) * _silu(z_ref[...].astype(F32))
    out = _dot(hh.astype(BF16), wout_ref[...])
    hnew = h_ref[...] + _rms(out, gpost_ref[...])
    hout_ref[...] = jnp.where(_row_keep(hnew.shape, TM), hnew, 0.0)


def _odd_out(layer, i, hf, hb, xc, z, hres, wo, bo, hn, sk, wout, gpost):
    lp = hres.shape[0]
    row = lambda w: pl.BlockSpec((TM, w), lambda i: (i, 0))
    _full, _full_l = _slab(i), _slab(layer)
    return pl.pallas_call(
        _odd_out_kernel,
        grid=(lp // TM,),
        in_specs=[row(ML_W), row(ML_W), row(ML_W), row(ML_W), row(D_MODEL),
                  _full((ML_W, ML_W)), _full((1, ML_W)), _full((1, ML_W)), _full((1, ML_W)),
                  _full((ML_W, D_MODEL)), _full_l((1, D_MODEL))],
        out_specs=row(D_MODEL),
        out_shape=jax.ShapeDtypeStruct((lp, D_MODEL), F32),
        compiler_params=_cparams("parallel"),
        name="odd_out",
    )(hf, hb, xc, z, hres, wo, bo, hn, sk, wout, gpost)


def _rope_tables(n_real):
    rows = n_real // GRID_W
    freqs = ROPE_THETA ** (-jnp.arange(0, ROPE_AXIS, 2, dtype=F32) / ROPE_AXIS)
    half = freqs.shape[0]
    ang_r = jnp.arange(rows, dtype=F32)[:, None] * freqs[None]
    ang_c = jnp.arange(GRID_W, dtype=F32)[:, None] * freqs[None]
    per_row = lambda t: jnp.broadcast_to(t[:, None, :], (rows, GRID_W, half)).reshape(n_real, half)
    per_col = lambda t: jnp.broadcast_to(t[None, :, :], (rows, GRID_W, half)).reshape(n_real, half)
    cr, sr = per_row(jnp.cos(ang_r)), per_row(jnp.sin(ang_r))
    cc, sc = per_col(jnp.cos(ang_c)), per_col(jnp.sin(ang_c))
    cos = jnp.concatenate([cr, cr, cc, cc], axis=1)
    sin = jnp.concatenate([-sr, sr, -sc, sc], axis=1)
    cos = jnp.concatenate([jnp.ones((FRONT, HEAD_DIM), F32), cos], axis=0)
    sin = jnp.concatenate([jnp.zeros((FRONT, HEAD_DIM), F32), sin], axis=0)
    return cos, sin


def kernel(x, meta_tokens, norm_pre, norm_post, w_in_even, q_norm, k_norm, ssm_a_re, ssm_a_im, ssm_log_dt, ssm_b_re, ssm_b_im, ssm_c_re, ssm_c_im, ssm_d, w_glu, b_glu, w_out_even, w_in_odd, conv_w, conv_b, w_q, w_k, w_v, w_igate, b_igate, w_fgate, b_fgate, w_ogate, b_ogate, head_norm, mlstm_skip, w_out_odd):
    bsz, n_real, _ = x.shape
    assert bsz == 1 and n_real % TK == 0 and (n_real + FRONT) % TM == 0
    lp = n_real + FRONT
    depth = norm_pre.shape[0]
    hres = jnp.concatenate([jnp.zeros((N_DUMMY, D_MODEL), F32), meta_tokens.astype(F32), x[0]], axis=0)
    cos, sin = _rope_tables(n_real)
    n_chunks_pad = -(-(lp // S5_T) // 16) * 16

    bf = lambda a: a.astype(BF16)
    vec = lambda a: a.reshape(a.shape[0], 1, -1)
    norm_pre, norm_post = vec(norm_pre), vec(norm_post)
    w_in_even = bf(w_in_even)
    wvt = jnp.swapaxes(w_in_even[:, :, ATTN_W + KV_W:ATTN_W + 2 * KV_W], 1, 2)
    qg = vec(q_norm) * (HEAD_DIM ** -0.5 * math.log2(math.e))
    kmat, bmat, cmat, a_pow = jax.vmap(_s5_tables)(ssm_a_re, ssm_a_im, ssm_log_dt, ssm_b_re, ssm_b_im,
                                                   ssm_c_re, ssm_c_im, ssm_d)
    w_glu, w_out_even = bf(w_glu), bf(w_out_even)
    w_in_odd, w_ogate, w_out_odd, w_v = bf(w_in_odd), bf(w_ogate), bf(w_out_odd), bf(w_v)
    wqk = bf(jnp.concatenate([w_q, w_k], axis=3))
    gate_pad = GATE_LANES - 4 * ML_H
    wg = jnp.concatenate([w_igate[:, 0], w_igate[:, 1], w_fgate[:, 0], w_fgate[:, 1]], axis=2)
    wg = bf(jnp.pad(wg, ((0, 0), (0, 0), (0, gate_pad))))
    bg = jnp.concatenate([b_igate[:, 0], b_igate[:, 1], b_fgate[:, 0], b_fgate[:, 1]], axis=1)
    bg = vec(jnp.pad(bg, ((0, 0), (0, gate_pad))))

    for layer in range(depth):
        i = layer // 2
        if layer % 2 == 0:
            q, k, vt, ga, u, gb = _even_in(layer, i, hres, norm_pre, w_in_even, wvt, qg, vec(k_norm), cos, sin)
            logit_bound = (math.sqrt(HEAD_DIM) * math.log2(math.e)
                           * jnp.max(jnp.abs(q_norm[i])) * jnp.max(jnp.abs(k_norm[i])))
            o = lax.cond(logit_bound <= LOGIT_SAFE,
                         lambda q, k, vt: _attention(q, k, vt, False),
                         lambda q, k, vt: _attention(q, k, vt, True), q, k, vt)
            y = _s5_unpack(_s5(i, _s5_pack(u, n_chunks_pad), kmat, bmat, cmat, a_pow), lp)
            hres = _even_out(layer, i, o, ga, y, gb, hres, w_glu, vec(b_glu), w_out_even, norm_post)
        else:
            xm, z = _odd_in(layer, i, hres, norm_pre, w_in_odd)
            qm, km, vm, xc, gates = _odd_mid(i, xm, conv_w, vec(conv_b), wqk, w_v, wg, bg)
            hf, hb = _mlstm(qm, km, vm, gates)
            hres = _odd_out(layer, i, hf, hb, xc, z, hres, w_ogate, vec(b_ogate), vec(head_norm), vec(mlstm_skip),
                            w_out_odd, norm_post)
    return hres[FRONT:].reshape(1, n_real, D_MODEL)
```

```python
import functools
import math

import jax
import jax.numpy as jnp
from jax import lax
from jax.experimental import pallas as pl
from jax.experimental.pallas import tpu as pltpu

F32 = jnp.float32
BF16 = jnp.bfloat16

D_MODEL = 1024
N_META = 16
FRONT = 128
N_DUMMY = FRONT - N_META
GRID_W = 64
EPS = 1e-6
NEG = -1e30

HEADS = 8
KV_HEADS = 2
KV_REP = HEADS // KV_HEADS
HEAD_DIM = 128
ROPE_AXIS = HEAD_DIM // 2
ROPE_THETA = 10000.0
ATTN_W = HEADS * HEAD_DIM
KV_W = KV_HEADS * HEAD_DIM
TQ = 384
TQS = 128
TK = 4096
VT_ROWS = HEAD_DIM + 16
LOGIT_SAFE = 60.0

S5_W = 1024
S5_P = 16
S5_G = S5_W // S5_P
S5_N = 64
S5_T = 16
S5_GB = 8
S5_PB = S5_GB // 2
S5_CW = S5_T * S5_P
S5_UNROLL = 8

ML_W = 2048
ML_H = 8
ML_DV = 256
ML_DK = 128
CONV_W = 5
ML_CHUNK = 128
ML_STEP_CHUNKS = 3
ML_AUG = 16
NEG_BIG = -1e9

TM = 384
VMEM_LIMIT = 56 * 1024 * 1024


def _cparams(*sem):
    return pltpu.CompilerParams(dimension_semantics=sem, vmem_limit_bytes=VMEM_LIMIT)


def _rms(x, g):
    return x * lax.rsqrt(jnp.mean(x * x, axis=-1, keepdims=True) + EPS) * g


def _sigmoid(x):
    return 0.5 * jnp.tanh(0.5 * x) + 0.5


def _silu(x):
    return x * _sigmoid(x)


def _gelu_tanh(x):
    return 0.5 * x * (1.0 + jnp.tanh(math.sqrt(2.0 / math.pi) * (x + 0.044715 * (x * x * x))))


def _dot(a, b):
    return jnp.dot(a, b, preferred_element_type=F32)


def _dot_nt(a, b):
    return lax.dot_general(a, b, (((1,), (1,)), ((), ())), preferred_element_type=F32)


def _dot_tn(a, b):
    return lax.dot_general(a, b, (((0,), (0,)), ((), ())), preferred_element_type=F32)


def _slab(layer):
    def spec(shape):
        return pl.BlockSpec((None,) + tuple(shape), lambda *_: (layer,) + (0,) * len(shape))
    return spec


def _even_in_kernel(h_ref, gpre_ref, w_ref, wvt_ref, qg_ref, kg_ref, cos_ref, sin_ref,
                    q_ref, k_ref, vt_ref, ga_ref, u_ref, gb_ref):
    hn = _rms(h_ref[...], gpre_ref[...]).astype(BF16)
    cos = cos_ref[...]
    sin = sin_ref[...]
    lane = lax.broadcasted_iota(jnp.int32, cos.shape, 1)
    first_half = (lane % ROPE_AXIS) < (ROPE_AXIS // 2)

    def head(x, g):
        y = _rms(x, g)
        partner = jnp.where(first_half, pltpu.roll(y, HEAD_DIM - ROPE_AXIS // 2, 1),
                            pltpu.roll(y, ROPE_AXIS // 2, 1))
        return (y * cos + partner * sin).astype(BF16)

    qsec = _dot(hn, w_ref[:, 0:ATTN_W])
    for h in range(HEADS):
        sl = slice(h * HEAD_DIM, (h + 1) * HEAD_DIM)
        q_ref[:, sl] = head(qsec[:, sl], qg_ref[...])
    ksec = _dot(hn, w_ref[:, ATTN_W:ATTN_W + KV_W])
    for h in range(KV_HEADS):
        sl = slice(h * HEAD_DIM, (h + 1) * HEAD_DIM)
        k_ref[:, sl] = head(ksec[:, sl], kg_ref[...])
    vt = _dot_nt(wvt_ref[...], hn).astype(BF16)
    for g in range(KV_HEADS):
        vt_ref[g * VT_ROWS:g * VT_ROWS + HEAD_DIM, :] = vt[g * HEAD_DIM:(g + 1) * HEAD_DIM]
        vt_ref[g * VT_ROWS + HEAD_DIM:(g + 1) * VT_ROWS, :] = jnp.ones((VT_ROWS - HEAD_DIM, TM), BF16)
    c0 = ATTN_W + 2 * KV_W
    ga_ref[...] = _dot(hn, w_ref[:, c0:c0 + ATTN_W]).astype(BF16)
    c0 += ATTN_W
    u_ref[...] = _dot(hn, w_ref[:, c0:c0 + S5_W])
    c0 += S5_W
    gb_ref[...] = _dot(hn, w_ref[:, c0:c0 + S5_W]).astype(BF16)


def _even_in(layer, i, hres, gpre, w_in, wvt, qg, kg, cos, sin):
    lp = hres.shape[0]
    win_w = w_in.shape[-1]
    _full, _full_l = _slab(i), _slab(layer)
    row = lambda w: pl.BlockSpec((TM, w), lambda i: (i, 0))
    vt_rows = KV_HEADS * VT_ROWS
    col = pl.BlockSpec((vt_rows, TM), lambda i: (0, i))
    widths = [ATTN_W, KV_W, None, ATTN_W, S5_W, S5_W]
    return pl.pallas_call(
        _even_in_kernel,
        grid=(lp // TM,),
        in_specs=[row(D_MODEL), _full_l((1, D_MODEL)), _full((D_MODEL, win_w)), _full((KV_W, D_MODEL)),
                  _full((1, HEAD_DIM)), _full((1, HEAD_DIM)), row(HEAD_DIM), row(HEAD_DIM)],
        out_specs=[col if w is None else row(w) for w in widths],
        out_shape=[jax.ShapeDtypeStruct((vt_rows, lp) if w is None else (lp, w), F32 if n == 4 else BF16)
                   for n, w in enumerate(widths)],
        compiler_params=_cparams("parallel"),
        name="even_in",
    )(hres, gpre, w_in, wvt, qg, kg, cos, sin)


def _attn_kernel(q_ref, k_ref, vt_ref, o_ref, m_sc, acc_sc, p_sc, *, n_kv_tiles, stabilised):
    qs = [jnp.concatenate([q_ref[a * TQS:(a + 1) * TQS, r * HEAD_DIM:(r + 1) * HEAD_DIM] for r in range(KV_REP)],
                          axis=0) for a in range(TQ // TQS)]

    def scores(kb, a, mask_dummy=False):
        st = _dot_nt(kb, qs[a])
        if mask_dummy:
            key = lax.broadcasted_iota(jnp.int32, st.shape, 0)
            st = jnp.where(key >= N_DUMMY, st, NEG)
        return st

    def tile_start(j):
        return pl.multiple_of(FRONT + j * TK, 128)

    k0, vt0 = k_ref[0:FRONT, :], vt_ref[:, 0:FRONT]
    if stabilised:
        def update(kb, vtb, mask_dummy):
            for a in range(len(qs)):
                st = scores(kb, a, mask_dummy)
                m_old = m_sc[a]
                m_new = jnp.maximum(m_old, jnp.max(st, axis=0, keepdims=True))
                p = jnp.exp2(st - m_new).astype(BF16)
                acc_sc[a] = jnp.exp2(m_old - m_new) * acc_sc[a] + _dot(vtb, p)
                m_sc[a] = m_new

        m_sc[...] = jnp.full(m_sc.shape, NEG, F32)
        acc_sc[...] = jnp.zeros(acc_sc.shape, F32)
        update(k0, vt0, True)

        def body(j, carry):
            update(k_ref[pl.ds(tile_start(j), TK), :], vt_ref[:, pl.ds(tile_start(j), TK)], False)
            return carry

        lax.fori_loop(0, n_kv_tiles, body, 0)
    else:
        for a in range(len(qs)):
            acc_sc[a] = _dot(vt0, jnp.exp2(scores(k0, a, True).astype(BF16)))
            p_sc[0, a] = jnp.exp2(scores(k_ref[FRONT:FRONT + TK, :], a).astype(BF16))

        def body(j, carry):
            cur, prev = j % 2, (j + 1) % 2
            kb = k_ref[pl.ds(tile_start(j), TK), :]
            vtb = vt_ref[:, pl.ds(tile_start(j - 1), TK)]
            for a in range(len(qs)):
                acc_sc[a] += _dot(vtb, p_sc[prev, a])
            for a in range(len(qs)):
                p_sc[cur, a] = jnp.exp2(scores(kb, a).astype(BF16))
            return carry

        lax.fori_loop(1, n_kv_tiles, body, 0)
        last = (n_kv_tiles - 1) % 2
        vtb = vt_ref[:, FRONT + (n_kv_tiles - 1) * TK:FRONT + n_kv_tiles * TK]
        for a in range(len(qs)):
            acc_sc[a] += _dot(vtb, p_sc[last, a])

    for a in range(TQ // TQS):
        ot = acc_sc[a, 0:HEAD_DIM, :] / acc_sc[a, HEAD_DIM:HEAD_DIM + 1, :]
        for r in range(KV_REP):
            o_ref[a * TQS:(a + 1) * TQS, r * HEAD_DIM:(r + 1) * HEAD_DIM] = (
                ot[:, r * TQS:(r + 1) * TQS].T.astype(BF16))


def _attention(q, k, vt, stabilised):
    lp = q.shape[0]
    n_kv_tiles = (lp - FRONT) // TK
    gw = KV_REP * HEAD_DIM
    n_sub = TQ // TQS
    return pl.pallas_call(
        functools.partial(_attn_kernel, n_kv_tiles=n_kv_tiles, stabilised=stabilised),
        grid=(KV_HEADS, lp // TQ),
        in_specs=[pl.BlockSpec((TQ, gw), lambda g, i: (i, g)),
                  pl.BlockSpec((lp, HEAD_DIM), lambda g, i: (0, g)),
                  pl.BlockSpec((VT_ROWS, lp), lambda g, i: (g, 0))],
        out_specs=pl.BlockSpec((TQ, gw), lambda g, i: (i, g)),
        out_shape=jax.ShapeDtypeStruct((lp, ATTN_W), BF16),
        scratch_shapes=[pltpu.VMEM((n_sub, 1, KV_REP * TQS), F32),
                        pltpu.VMEM((n_sub, VT_ROWS, KV_REP * TQS), F32),
                        pltpu.VMEM((2, n_sub, TK, KV_REP * TQS), BF16)],
        compiler_params=_cparams("parallel", "parallel"),
        name="attention_stabilised" if stabilised else "attention",
    )(q, k, vt)


def _s5_tables(a_re, a_im, log_dt, b_re, b_im, c_re, c_im, d_skip):
    t = S5_T
    lag = jnp.arange(t + 1, dtype=F32)
    per_dir = []
    for d in range(2):
        a = lax.complex(a_re[d], a_im[d])
        adt = a * jnp.exp(log_dt[d])[:, None]
        a_bar = jnp.exp(adt)
        b_bar = ((a_bar - 1.0) / a)[..., None] * lax.complex(b_re[d], b_im[d])
        c = lax.complex(c_re[d], c_im[d])
        pw = jnp.exp(adt[None] * lag[:, None, None].astype(jnp.complex64))
        klag = jnp.real(jnp.einsum('gon,dgn,gni->dgoi', c, pw[:t], b_bar,
                                   precision=lax.Precision.HIGHEST))
        per_dir.append((b_bar, c, pw, klag))
    (bb0, c0, pw0, kl0), (bb1, c1, pw1, kl1) = per_dir
    ii = jnp.arange(t)[:, None, None]
    jj = jnp.arange(t)[None, :, None]
    dd = jnp.arange(t)[None, None, :]
    place_f = (jj - ii == dd).astype(F32)
    place_b = (ii - jj == dd).astype(F32)
    skip = d_skip.reshape(S5_G, S5_P)[:, :, None] * jnp.eye(S5_P, dtype=F32)
    kl0 = kl0.at[0].add(skip)
    kmat = (jnp.einsum('ijd,dgop->gipjo', place_f, kl0, precision=lax.Precision.HIGHEST)
            + jnp.einsum('ijd,dgop->gipjo', place_b, kl1, precision=lax.Precision.HIGHEST)
            ).reshape(S5_G, S5_CW, S5_CW)

    bf = pw0[:t][::-1][..., None] * bb0[None]
    bb = pw1[:t][..., None] * bb1[None]
    cf = c0[None] * pw0[1:t + 1][:, :, None, :]
    cb = c1[None] * pw1[1:t + 1][::-1][:, :, None, :]

    def in_mat(x):
        return x.astype(BF16).transpose(1, 0, 3, 2).reshape(S5_G, S5_CW, S5_N)

    def out_mat(x):
        return x.astype(BF16).transpose(1, 3, 0, 2).reshape(S5_G, S5_N, S5_CW)

    n_pair = S5_G // 2
    zero_in = jnp.zeros((n_pair, S5_CW, S5_N), BF16)
    zero_out = jnp.zeros((n_pair, S5_N, S5_CW), BF16)

    def pair_in(x):
        e, o = x[0::2], x[1::2]
        return jnp.concatenate([jnp.concatenate([e, zero_in], axis=2),
                                jnp.concatenate([zero_in, o], axis=2)], axis=1)

    def pair_out(x):
        e, o = x[0::2], x[1::2]
        return jnp.concatenate([jnp.concatenate([e, zero_out], axis=2),
                                jnp.concatenate([zero_out, o], axis=2)], axis=1)

    ins = [in_mat(jnp.real(bf)), in_mat(jnp.imag(bf)), in_mat(jnp.real(bb)), in_mat(jnp.imag(bb))]
    outs = [out_mat(jnp.real(cf)), out_mat(-jnp.imag(cf)), out_mat(jnp.real(cb)), out_mat(-jnp.imag(cb))]
    ins, outs = lax.optimization_barrier((ins, outs))
    bmat = jnp.concatenate([pair_in(m) for m in ins], axis=2)
    cmat = jnp.concatenate([pair_out(m) for m in outs], axis=1)

    def pair_row(x):
        return x.reshape(S5_G // S5_GB, 1, S5_PB * 2 * S5_N)

    a_pow = jnp.concatenate([pair_row(jnp.real(pw0[t])), pair_row(jnp.imag(pw0[t])),
                             pair_row(jnp.real(pw1[t])), pair_row(jnp.imag(pw1[t]))], axis=1)
    return kmat.astype(BF16), bmat, cmat, a_pow


LANE_GROUPS = 128 // S5_P


def _swap_token_group(xs):
    block = lax.broadcasted_iota(jnp.int32, xs[0].shape, 1) // S5_P
    xs = list(xs)
    d = LANE_GROUPS // 2
    while d:
        upper = (block & d) != 0
        for a in range(LANE_GROUPS):
            if a & d:
                continue
            lo, hi = xs[a], xs[a + d]
            xs[a] = jnp.where(upper, pltpu.roll(hi, S5_P * d, 1), lo)
            xs[a + d] = jnp.where(upper, hi, pltpu.roll(lo, 128 - S5_P * d, 1))
        d //= 2
    return xs


def _s5_pack_kernel(u_ref, o_ref, *, n_chunks):
    n_main = n_chunks // 16 * 16
    for half in range(S5_T // LANE_GROUPS):
        xs = [u_ref[pl.ds(LANE_GROUPS * half + i, n_chunks, stride=S5_T), :] for i in range(LANE_GROUPS)]
        for g, y in enumerate(_swap_token_group(xs)):
            cols = slice(g * S5_CW + half * 128, g * S5_CW + (half + 1) * 128)
            o_ref[0:n_main, cols] = y[0:n_main].astype(BF16)
            if o_ref.shape[0] > n_main:
                tail = jnp.concatenate([y[n_main:], jnp.zeros((o_ref.shape[0] - n_chunks, 128), F32)], axis=0)
                o_ref[n_main:, cols] = tail.astype(BF16)


def _s5_unpack_kernel(y_ref, o_ref, *, n_chunks):
    for half in range(S5_T // LANE_GROUPS):
        ys = [y_ref[:, g * S5_CW + half * 128:g * S5_CW + (half + 1) * 128].astype(F32)[0:n_chunks]
              for g in range(LANE_GROUPS)]
        for i, x in enumerate(_swap_token_group(ys)):
            o_ref[pl.ds(LANE_GROUPS * half + i, n_chunks, stride=S5_T), :] = x


def _s5_pack(u, n_chunks_pad):
    lp = u.shape[0]
    bw = LANE_GROUPS * S5_CW
    return pl.pallas_call(
        functools.partial(_s5_pack_kernel, n_chunks=lp // S5_T),
        grid=(S5_W // 128,),
        in_specs=[pl.BlockSpec((lp, 128), lambda i: (0, i))],
        out_specs=pl.BlockSpec((n_chunks_pad, bw), lambda i: (0, i)),
        out_shape=jax.ShapeDtypeStruct((n_chunks_pad, S5_G * S5_CW), BF16),
        compiler_params=_cparams("parallel"),
        name="s5_pack",
    )(u)


def _s5_unpack(yc, lp):
    bw = LANE_GROUPS * S5_CW
    return pl.pallas_call(
        functools.partial(_s5_unpack_kernel, n_chunks=lp // S5_T),
        grid=(S5_W // 128,),
        in_specs=[pl.BlockSpec((yc.shape[0], bw), lambda i: (0, i))],
        out_specs=pl.BlockSpec((lp, 128), lambda i: (0, i)),
        out_shape=jax.ShapeDtypeStruct((lp, S5_W), F32),
        compiler_params=_cparams("parallel"),
        name="s5_unpack",
    )(yc)


def _s5_kernel(u_ref, k_ref, b_ref, c_ref, a_ref, y_ref, fre, fim, bre, bim, *, n_chunks):
    pw = 2 * S5_N
    for p in range(S5_PB):
        s = _dot(u_ref[:, p * 2 * S5_CW:(p + 1) * 2 * S5_CW], b_ref[p])
        cols = slice(p * pw, (p + 1) * pw)
        fre[:, cols] = s[:, 0:pw]
        fim[:, cols] = s[:, pw:2 * pw]
        bre[:, cols] = s[:, 2 * pw:3 * pw]
        bim[:, cols] = s[:, 3 * pw:4 * pw]

    far, fai, bar, bai = a_ref[0, 0:1, :], a_ref[0, 1:2, :], a_ref[0, 2:3, :], a_ref[0, 3:4, :]

    def step(i, carry):
        xr, xi, yr, yi = carry
        rf = pl.ds(i, 1)
        rb = pl.ds(n_chunks - 1 - i, 1)
        sr, si = fre[rf, :], fim[rf, :]
        tr, ti = bre[rb, :], bim[rb, :]
        fre[rf, :] = xr
        fim[rf, :] = xi
        bre[rb, :] = yr
        bim[rb, :] = yi
        return (far * xr - fai * xi + sr, far * xi + fai * xr + si,
                bar * yr - bai * yi + tr, bar * yi + bai * yr + ti)

    zero = jnp.zeros((1, S5_PB * pw), F32)
    lax.fori_loop(0, n_chunks, step, (zero, zero, zero, zero), unroll=S5_UNROLL)

    for p in range(S5_PB):
        cols = slice(p * pw, (p + 1) * pw)
        state = jnp.concatenate([fre[:, cols], fim[:, cols], bre[:, cols], bim[:, cols]], axis=1).astype(BF16)
        carry_out = _dot(state, c_ref[p])
        for e in range(2):
            g = 2 * p + e
            gc = slice(g * S5_CW, (g + 1) * S5_CW)
            y = _dot(u_ref[:, gc], k_ref[g]) + carry_out[:, e * S5_CW:(e + 1) * S5_CW]
            y_ref[:, gc] = y.astype(BF16)


def _s5(layer, u_chunks, kmat, bmat, cmat, a_pow):
    n_chunks = u_chunks.shape[0]
    bw = S5_GB * S5_CW
    sw = S5_PB * 2 * S5_N
    return pl.pallas_call(
        functools.partial(_s5_kernel, n_chunks=n_chunks),
        grid=(S5_G // S5_GB,),
        in_specs=[pl.BlockSpec((n_chunks, bw), lambda i: (0, i)),
                  pl.BlockSpec((None, S5_GB, S5_CW, S5_CW), lambda i: (layer, i, 0, 0)),
                  pl.BlockSpec((None, S5_PB, 2 * S5_CW, 8 * S5_N), lambda i: (layer, i, 0, 0)),
                  pl.BlockSpec((None, S5_PB, 8 * S5_N, 2 * S5_CW), lambda i: (layer, i, 0, 0)),
                  pl.BlockSpec((None, 1, 4, sw), lambda i: (layer, i, 0, 0))],
        out_specs=pl.BlockSpec((n_chunks, bw), lambda i: (0, i)),
        out_shape=jax.ShapeDtypeStruct(u_chunks.shape, BF16),
        scratch_shapes=[pltpu.VMEM((n_chunks, sw), F32)] * 4,
        compiler_params=_cparams("parallel"),
        name="s5_scan",
    )(u_chunks, kmat, bmat, cmat, a_pow)


def _row_keep(shape, tile_rows):
    row = pl.program_id(0) * tile_rows + lax.broadcasted_iota(jnp.int32, shape, 0)
    return row >= N_DUMMY


def _even_out_kernel(o_ref, ga_ref, y_ref, gb_ref, h_ref, wglu_ref, bglu_ref, wout_ref, gpost_ref, hout_ref):
    ya = o_ref[...].astype(F32) * _silu(ga_ref[...].astype(F32))
    yb = _gelu_tanh(y_ref[...].astype(F32))
    yb = yb * _sigmoid(_dot(yb.astype(BF16), wglu_ref[...]) + bglu_ref[...])
    yb = yb * _silu(gb_ref[...].astype(F32))
    out = _dot(ya.astype(BF16), wout_ref[0:ATTN_W, :]) + _dot(yb.astype(BF16), wout_ref[ATTN_W:, :])
    hnew = h_ref[...] + _rms(out, gpost_ref[...])
    hout_ref[...] = jnp.where(_row_keep(hnew.shape, TM), hnew, 0.0)


def _even_out(layer, i, o, ga, y, gb, hres, wglu, bglu, wout, gpost):
    lp = hres.shape[0]
    row = lambda w: pl.BlockSpec((TM, w), lambda i: (i, 0))
    _full, _full_l = _slab(i), _slab(layer)
    return pl.pallas_call(
        _even_out_kernel,
        grid=(lp // TM,),
        in_specs=[row(ATTN_W), row(ATTN_W), row(S5_W), row(S5_W), row(D_MODEL),
                  _full((S5_W, S5_W)), _full((1, S5_W)), _full((ATTN_W + S5_W, D_MODEL)), _full_l((1, D_MODEL))],
        out_specs=row(D_MODEL),
        out_shape=jax.ShapeDtypeStruct((lp, D_MODEL), F32),
        compiler_params=_cparams("parallel"),
        name="even_out",
    )(o, ga, y, gb, hres, wglu, bglu, wout, gpost)


def _odd_in_kernel(h_ref, gpre_ref, w_ref, xm_ref, z_ref):
    hn = _rms(h_ref[...], gpre_ref[...]).astype(BF16)
    xm_ref[...] = _dot(hn, w_ref[:, 0:ML_W]).astype(BF16)
    z_ref[...] = _dot(hn, w_ref[:, ML_W:]).astype(BF16)


def _odd_in(layer, i, hres, gpre, w_in):
    lp = hres.shape[0]
    row = lambda w: pl.BlockSpec((TM, w), lambda i: (i, 0))
    return pl.pallas_call(
        _odd_in_kernel,
        grid=(lp // TM,),
        in_specs=[row(D_MODEL), _slab(layer)((1, D_MODEL)), _slab(i)((D_MODEL, 2 * ML_W))],
        out_specs=[row(ML_W), row(ML_W)],
        out_shape=[jax.ShapeDtypeStruct((lp, ML_W), BF16)] * 2,
        compiler_params=_cparams("parallel"),
        name="odd_in",
    )(hres, gpre, w_in)


HALO = 16
CONV_BLOCK = 128
GATE_LANES = 128


def _conv_shift_matrix():
    offs = [j - CONV_W // 2 for j in range(CONV_W) if j != CONV_W // 2]
    r = jnp.arange(CONV_BLOCK)[None, :, None]
    c = jnp.arange(CONV_BLOCK + 2 * HALO)[None, None, :]
    off = jnp.asarray(offs)[:, None, None]
    return (c == r + HALO + off).astype(BF16).reshape(len(offs) * CONV_BLOCK, CONV_BLOCK + 2 * HALO)


def _odd_mid_kernel(xm_ref, prev_ref, next_ref, sh_ref, cw_ref, cb_ref, wqk_ref, wv_ref, wg_ref, bg_ref,
                    q_ref, k_ref, v_ref, xc_ref, g_ref, ext):
    i = pl.program_id(0)
    n = pl.num_programs(0)
    xm = xm_ref[...]
    ext[0:HALO, :] = jnp.where(i > 0, prev_ref[...], jnp.zeros((), BF16))
    ext[HALO:HALO + TM, :] = xm
    ext[HALO + TM:, :] = jnp.where(i < n - 1, next_ref[...], jnp.zeros((), BF16))
    centre = CONV_W // 2
    for b in range(TM // CONV_BLOCK):
        rows = slice(b * CONV_BLOCK, (b + 1) * CONV_BLOCK)
        shifted = _dot(sh_ref[...], ext[b * CONV_BLOCK:(b + 1) * CONV_BLOCK + 2 * HALO, :])
        conv = cb_ref[...] + cw_ref[centre:centre + 1, :] * xm[rows].astype(F32)
        for n_tap, j in enumerate(j for j in range(CONV_W) if j != centre):
            conv = conv + cw_ref[j:j + 1, :] * shifted[n_tap * CONV_BLOCK:(n_tap + 1) * CONV_BLOCK]
        xc_ref[rows, :] = _silu(conv).astype(BF16)
    xcb = xc_ref[...]

    for h in range(ML_H):
        cin = slice(h * ML_DV, (h + 1) * ML_DV)
        qk = _dot(xcb[:, cin], wqk_ref[h])
        q_ref[:, h * ML_DK:(h + 1) * ML_DK] = (qk[:, 0:ML_DK] * (ML_DK ** -0.5)).astype(BF16)
        k_ref[:, h * ML_DK:(h + 1) * ML_DK] = qk[:, ML_DK:].astype(BF16)
        v_ref[:, cin] = _dot(xm[:, cin], wv_ref[h]).astype(BF16)

    pre = _dot(xcb, wg_ref[...]) + bg_ref[...]
    lane = lax.broadcasted_iota(jnp.int32, pre.shape, 1)
    keep = _row_keep(pre.shape, TM)
    li = jnp.where(keep, pre, NEG_BIG)
    lf = jnp.where(keep, jnp.minimum(pre, 0.0) - jnp.log(1.0 + jnp.exp(-jnp.abs(pre))), 0.0)
    t = lax.broadcasted_iota(jnp.int32, (TM, TM), 0)
    s = lax.broadcasted_iota(jnp.int32, (TM, TM), 1)
    same = (t // ML_CHUNK) == (s // ML_CHUNK)
    tri_f = jnp.where(same & (s <= t), 1.0, 0.0).astype(BF16)
    tri_b = jnp.where(same & (s >= t), 1.0, 0.0).astype(BF16)
    lf_hi = lf.astype(BF16)
    lf_lo = (lf - lf_hi.astype(F32)).astype(BF16)
    cum_f = _dot(tri_f, lf_hi) + _dot(tri_f, lf_lo)
    cum_b = _dot(tri_b, lf_hi) + _dot(tri_b, lf_lo)
    g_ref[...] = jnp.where(lane < 2 * ML_H, li, jnp.where(lane < 3 * ML_H, cum_f, cum_b))


def _odd_mid(i, xm, cw, cb, wqk, wv, wg, bg):
    lp = xm.shape[0]
    row = lambda w: pl.BlockSpec((TM, w), lambda i: (i, 0))
    _full = _slab(i)
    per = TM // HALO
    last = lp // HALO - 1
    shift = _conv_shift_matrix()
    return pl.pallas_call(
        _odd_mid_kernel,
        grid=(lp // TM,),
        in_specs=[row(ML_W),
                  pl.BlockSpec((HALO, ML_W), lambda i: (jnp.maximum(i * per - 1, 0), 0)),
                  pl.BlockSpec((HALO, ML_W), lambda i: (jnp.minimum((i + 1) * per, last), 0)),
                  pl.BlockSpec(shift.shape, lambda i: (0, 0)),
                  _full((CONV_W, ML_W)), _full((1, ML_W)),
                  _full((ML_H, ML_DV, 2 * ML_DK)), _full((ML_H, ML_DV, ML_DV)),
                  _full((ML_W, GATE_LANES)), _full((1, GATE_LANES))],
        out_specs=[row(ML_H * ML_DK), row(ML_H * ML_DK), row(ML_W), row(ML_W), row(GATE_LANES)],
        out_shape=[jax.ShapeDtypeStruct((lp, ML_H * ML_DK), BF16), jax.ShapeDtypeStruct((lp, ML_H * ML_DK), BF16),
                   jax.ShapeDtypeStruct((lp, ML_W), BF16), jax.ShapeDtypeStruct((lp, ML_W), BF16),
                   jax.ShapeDtypeStruct((lp, GATE_LANES), F32)],
        scratch_shapes=[pltpu.VMEM((TM + 2 * HALO, ML_W), BF16)],
        compiler_params=_cparams("parallel"),
        name="odd_mid",
    )(xm, xm, xm, shift, cw, cb, wqk, wv, wg, bg)


def _mlstm_chunk(q, k, v, g, gt, h, backward, state, m_prev):
    c = ML_CHUNK
    off = ML_H if backward else 0
    li_row = gt[off + h:off + h + 1, :]
    b_row = gt[2 * ML_H + off + h:2 * ML_H + off + h + 1, :]
    a_row = li_row - b_row
    a_col = g[:, off + h:off + h + 1] - g[:, 2 * ML_H + off + h:2 * ML_H + off + h + 1]
    b_last = jnp.broadcast_to(b_row[:, 0:1] if backward else b_row[:, c - 1:c], (1, c))

    kq = _dot_nt(k, q)
    vt_aug = jnp.concatenate([v.T, jnp.ones((ML_AUG, c), BF16)], axis=0)
    qt = q.T.astype(F32)
    m_new = b_last + jnp.maximum(m_prev, jnp.max(a_row, axis=-1, keepdims=True))
    decay = jnp.exp(b_last + m_prev - m_new)
    w_row = jnp.exp(b_last + a_row - m_new)
    update = _dot((vt_aug.astype(F32) * w_row).astype(BF16), k)
    yield decay * state + update, m_new

    s = lax.broadcasted_iota(jnp.int32, (c, c), 0)
    t = lax.broadcasted_iota(jnp.int32, (c, c), 1)
    allowed = (s >= t) if backward else (s <= t)
    dt = jnp.where(allowed, b_row + a_col, NEG)
    inter_log = b_row + m_prev
    m_t = jnp.maximum(jnp.max(dt, axis=0, keepdims=True), inter_log)
    swt = jnp.exp(dt - m_t) * kq
    inter = jnp.exp(inter_log - m_t)
    num = _dot(jnp.concatenate([vt_aug, state.astype(BF16)], axis=1),
               jnp.concatenate([swt.astype(BF16), (qt * inter).astype(BF16)], axis=0))
    yield None

    den = num[ML_DV:ML_DV + 1, :]
    out_t = num[0:ML_DV, :] / jnp.maximum(jnp.abs(den), jnp.exp(-m_t))
    yield out_t.T


def _mlstm_kernel(qf_ref, kf_ref, vf_ref, gf_ref, qb_ref, kb_ref, vb_ref, gb_ref, hf_ref, hb_ref, c_sc, m_sc):
    @pl.when(pl.program_id(0) == 0)
    def _():
        c_sc[...] = jnp.zeros(c_sc.shape, F32)
        m_sc[...] = jnp.zeros(m_sc.shape, F32)

    chunks = []
    for backward, (q_ref, k_ref, v_ref, g_ref, h_ref) in enumerate(
            ((qf_ref, kf_ref, vf_ref, gf_ref, hf_ref), (qb_ref, kb_ref, vb_ref, gb_ref, hb_ref))):
        order = range(ML_STEP_CHUNKS - 1, -1, -1) if backward else range(ML_STEP_CHUNKS)
        gs = [g_ref[c * ML_CHUNK:(c + 1) * ML_CHUNK, :] for c in range(ML_STEP_CHUNKS)]
        gts = [g.T for g in gs]
        for h in range(ML_H):
            slot = backward * ML_H + h
            state, m = c_sc[slot], m_sc[slot]
            for c in order:
                rows = slice(c * ML_CHUNK, (c + 1) * ML_CHUNK)
                gen = _mlstm_chunk(q_ref[rows, h * ML_DK:(h + 1) * ML_DK], k_ref[rows, h * ML_DK:(h + 1) * ML_DK],
                                   v_ref[rows, h * ML_DV:(h + 1) * ML_DV], gs[c], gts[c], h, bool(backward),
                                   state, m)
                state, m = next(gen)
                chunks.append((gen, h_ref, rows, h))
            c_sc[slot] = state
            m_sc[slot] = m
    for gen, _, _, _ in chunks:
        next(gen)
    for gen, h_ref, rows, h in chunks:
        h_ref[rows, h * ML_DV:(h + 1) * ML_DV] = next(gen).astype(BF16)


def _mlstm(q, k, v, gates):
    lp = q.shape[0]
    rows = ML_STEP_CHUNKS * ML_CHUNK
    nch = lp // rows
    fwd = lambda w: pl.BlockSpec((rows, w), lambda i: (i, 0))
    bwd = lambda w: pl.BlockSpec((rows, w), lambda i: (nch - 1 - i, 0))
    qk_w = ML_H * ML_DK
    return pl.pallas_call(
        _mlstm_kernel,
        grid=(nch,),
        in_specs=[fwd(qk_w), fwd(qk_w), fwd(ML_W), fwd(GATE_LANES),
                  bwd(qk_w), bwd(qk_w), bwd(ML_W), bwd(GATE_LANES)],
        out_specs=[fwd(ML_W), bwd(ML_W)],
        out_shape=[jax.ShapeDtypeStruct((lp, ML_W), BF16)] * 2,
        scratch_shapes=[pltpu.VMEM((2 * ML_H, ML_DV + ML_AUG, ML_DK), F32),
                        pltpu.VMEM((2 * ML_H, 1, ML_CHUNK), F32)],
        compiler_params=_cparams("arbitrary"),
        name="mlstm",
    )(q, k, v, gates, q, k, v, gates)


def _odd_out_kernel(hf_ref, hb_ref, xc_ref, z_ref, h_ref, wo_ref, bo_ref, hn_ref, sk_ref, wout_ref, gpost_ref,
                    hout_ref):
    xcb = xc_ref[...]
    xc = xcb.astype(F32)
    og = _sigmoid(_dot(xcb, wo_ref[...]) + bo_ref[...])
    cell = hf_ref[...].astype(F32) + hb_ref[...].astype(F32)
    parts = []
    for h in range(ML_H):
        sl = slice(h * ML_DV, (h + 1) * ML_DV)
        parts.append(_rms(cell[:, sl], hn_ref[:, sl]))
    cell = jnp.concatenate(parts, axis=1)
    hh = (og * cell + sk_ref[...] * xc) * _silu(z_ref[...].astype(F32))
    out = _dot(hh.astype(BF16), wout_ref[...])
    hnew = h_ref[...] + _rms(out, gpost_ref[...])
    hout_ref[...] = jnp.where(_row_keep(hnew.shape, TM), hnew, 0.0)


def _odd_out(layer, i, hf, hb, xc, z, hres, wo, bo, hn, sk, wout, gpost):
    lp = hres.shape[0]
    row = lambda w: pl.BlockSpec((TM, w), lambda i: (i, 0))
    _full, _full_l = _slab(i), _slab(layer)
    return pl.pallas_call(
        _odd_out_kernel,
        grid=(lp // TM,),
        in_specs=[row(ML_W), row(ML_W), row(ML_W), row(ML_W), row(D_MODEL),
                  _full((ML_W, ML_W)), _full((1, ML_W)), _full((1, ML_W)), _full((1, ML_W)),
                  _full((ML_W, D_MODEL)), _full_l((1, D_MODEL))],
        out_specs=row(D_MODEL),
        out_shape=jax.ShapeDtypeStruct((lp, D_MODEL), F32),
        compiler_params=_cparams("parallel"),
        name="odd_out",
    )(hf, hb, xc, z, hres, wo, bo, hn, sk, wout, gpost)


def _rope_tables(n_real):
    rows = n_real // GRID_W
    freqs = ROPE_THETA ** (-jnp.arange(0, ROPE_AXIS, 2, dtype=F32) / ROPE_AXIS)
    half = freqs.shape[0]
    ang_r = jnp.arange(rows, dtype=F32)[:, None] * freqs[None]
    ang_c = jnp.arange(GRID_W, dtype=F32)[:, None] * freqs[None]
    per_row = lambda t: jnp.broadcast_to(t[:, None, :], (rows, GRID_W, half)).reshape(n_real, half)
    per_col = lambda t: jnp.broadcast_to(t[None, :, :], (rows, GRID_W, half)).reshape(n_real, half)
    cr, sr = per_row(jnp.cos(ang_r)), per_row(jnp.sin(ang_r))
    cc, sc = per_col(jnp.cos(ang_c)), per_col(jnp.sin(ang_c))
    cos = jnp.concatenate([cr, cr, cc, cc], axis=1)
    sin = jnp.concatenate([-sr, sr, -sc, sc], axis=1)
    cos = jnp.concatenate([jnp.ones((FRONT, HEAD_DIM), F32), cos], axis=0)
    sin = jnp.concatenate([jnp.zeros((FRONT, HEAD_DIM), F32), sin], axis=0)
    return cos, sin


def kernel(x, meta_tokens, norm_pre, norm_post, w_in_even, q_norm, k_norm, ssm_a_re, ssm_a_im, ssm_log_dt, ssm_b_re, ssm_b_im, ssm_c_re, ssm_c_im, ssm_d, w_glu, b_glu, w_out_even, w_in_odd, conv_w, conv_b, w_q, w_k, w_v, w_igate, b_igate, w_fgate, b_fgate, w_ogate, b_ogate, head_norm, mlstm_skip, w_out_odd):
    bsz, n_real, _ = x.shape
    assert bsz == 1 and n_real % TK == 0 and (n_real + FRONT) % TM == 0
    lp = n_real + FRONT
    depth = norm_pre.shape[0]
    hres = jnp.concatenate([jnp.zeros((N_DUMMY, D_MODEL), F32), meta_tokens.astype(F32), x[0]], axis=0)
    cos, sin = _rope_tables(n_real)
    n_chunks_pad = -(-(lp // S5_T) // 16) * 16

    bf = lambda a: a.astype(BF16)
    vec = lambda a: a.reshape(a.shape[0], 1, -1)
    norm_pre, norm_post = vec(norm_pre), vec(norm_post)
    w_in_even = bf(w_in_even)
    wvt = jnp.swapaxes(w_in_even[:, :, ATTN_W + KV_W:ATTN_W + 2 * KV_W], 1, 2)
    qg = vec(q_norm) * (HEAD_DIM ** -0.5 * math.log2(math.e))
    kmat, bmat, cmat, a_pow = jax.vmap(_s5_tables)(ssm_a_re, ssm_a_im, ssm_log_dt, ssm_b_re, ssm_b_im,
                                                   ssm_c_re, ssm_c_im, ssm_d)
    w_glu, w_out_even = bf(w_glu), bf(w_out_even)
    w_in_odd, w_ogate, w_out_odd, w_v = bf(w_in_odd), bf(w_ogate), bf(w_out_odd), bf(w_v)
    wqk = bf(jnp.concatenate([w_q, w_k], axis=3))
    gate_pad = GATE_LANES - 4 * ML_H
    wg = jnp.concatenate([w_igate[:, 0], w_igate[:, 1], w_fgate[:, 0], w_fgate[:, 1]], axis=2)
    wg = bf(jnp.pad(wg, ((0, 0), (0, 0), (0, gate_pad))))
    bg = jnp.concatenate([b_igate[:, 0], b_igate[:, 1], b_fgate[:, 0], b_fgate[:, 1]], axis=1)
    bg = vec(jnp.pad(bg, ((0, 0), (0, gate_pad))))

    for layer in range(depth):
        i = layer // 2
        if layer % 2 == 0:
            q, k, vt, ga, u, gb = _even_in(layer, i, hres, norm_pre, w_in_even, wvt, qg, vec(k_norm), cos, sin)
            logit_bound = (math.sqrt(HEAD_DIM) * math.log2(math.e)
                           * jnp.max(jnp.abs(q_norm[i])) * jnp.max(jnp.abs(k_norm[i])))
            o = lax.cond(logit_bound <= LOGIT_SAFE,
                         lambda q, k, vt: _attention(q, k, vt, False),
                         lambda q, k, vt: _attention(q, k, vt, True), q, k, vt)
            y = _s5_unpack(_s5(i, _s5_pack(u, n_chunks_pad), kmat, bmat, cmat, a_pow), lp)
            hres = _even_out(layer, i, o, ga, y, gb, hres, w_glu, vec(b_glu), w_out_even, norm_post)
        else:
            xm, z = _odd_in(layer, i, hres, norm_pre, w_in_odd)
            qm, km, vm, xc, gates = _odd_mid(i, xm, conv_w, vec(conv_b), wqk, w_v, wg, bg)
            hf, hb = _mlstm(qm, km, vm, gates)
            hres = _odd_out(layer, i, hf, hb, xc, z, hres, w_ogate, vec(b_ogate), vec(head_norm), vec(mlstm_skip),
                            w_out_odd, norm_post)
    return hres[FRONT:].reshape(1, n_real, D_MODEL)
```

```python
import functools
import math

import jax
import jax.numpy as jnp
from jax import lax
from jax.experimental import pallas as pl
from jax.experimental.pallas import tpu as pltpu

F32 = jnp.float32
BF16 = jnp.bfloat16

D_MODEL = 1024
N_META = 16
FRONT = 128
N_DUMMY = FRONT - N_META
GRID_W = 64
EPS = 1e-6
NEG = -1e30

HEADS = 8
KV_HEADS = 2
KV_REP = HEADS // KV_HEADS
HEAD_DIM = 128
ROPE_AXIS = HEAD_DIM // 2
ROPE_THETA = 10000.0
ATTN_W = HEADS * HEAD_DIM
KV_W = KV_HEADS * HEAD_DIM
TQ = 384
TQS = 128
TK = 4096
VT_ROWS = HEAD_DIM + 16
LOGIT_SAFE = 60.0

S5_W = 1024
S5_P = 16
S5_G = S5_W // S5_P
S5_N = 64
S5_T = 16
S5_GB = 8
S5_PB = S5_GB // 2
S5_CW = S5_T * S5_P
S5_UNROLL = 8

ML_W = 2048
ML_H = 8
ML_DV = 256
ML_DK = 128
CONV_W = 5
ML_CHUNK = 128
ML_STEP_CHUNKS = 3
ML_AUG = 16
NEG_BIG = -1e9

TM = 384
VMEM_LIMIT = 56 * 1024 * 1024


def _cparams(*sem):
    return pltpu.CompilerParams(dimension_semantics=sem, vmem_limit_bytes=VMEM_LIMIT)


def _rms(x, g):
    return x * lax.rsqrt(jnp.mean(x * x, axis=-1, keepdims=True) + EPS) * g


def _sigmoid(x):
    return 0.5 * jnp.tanh(0.5 * x) + 0.5


def _silu(x):
    return x * _sigmoid(x)


def _gelu_tanh(x):
    return 0.5 * x * (1.0 + jnp.tanh(math.sqrt(2.0 / math.pi) * (x + 0.044715 * (x * x * x))))


def _dot(a, b):
    return jnp.dot(a, b, preferred_element_type=F32)


def _dot_nt(a, b):
    return lax.dot_general(a, b, (((1,), (1,)), ((), ())), preferred_element_type=F32)


def _dot_tn(a, b):
    return lax.dot_general(a, b, (((0,), (0,)), ((), ())), preferred_element_type=F32)


def _slab(layer):
    def spec(shape):
        return pl.BlockSpec((None,) + tuple(shape), lambda *_: (layer,) + (0,) * len(shape))
    return spec


def _even_in_kernel(h_ref, gpre_ref, w_ref, wvt_ref, qg_ref, kg_ref, cos_ref, sin_ref,
                    q_ref, k_ref, vt_ref, ga_ref, u_ref, gb_ref):
    hn = _rms(h_ref[...], gpre_ref[...]).astype(BF16)
    cos = cos_ref[...]
    sin = sin_ref[...]
    lane = lax.broadcasted_iota(jnp.int32, cos.shape, 1)
    first_half = (lane % ROPE_AXIS) < (ROPE_AXIS // 2)

    def head(x, g):
        y = _rms(x, g)
        partner = jnp.where(first_half, pltpu.roll(y, HEAD_DIM - ROPE_AXIS // 2, 1),
                            pltpu.roll(y, ROPE_AXIS // 2, 1))
        return (y * cos + partner * sin).astype(BF16)

    qsec = _dot(hn, w_ref[:, 0:ATTN_W])
    for h in range(HEADS):
        sl = slice(h * HEAD_DIM, (h + 1) * HEAD_DIM)
        q_ref[:, sl] = head(qsec[:, sl], qg_ref[...])
    ksec = _dot(hn, w_ref[:, ATTN_W:ATTN_W + KV_W])
    for h in range(KV_HEADS):
        sl = slice(h * HEAD_DIM, (h + 1) * HEAD_DIM)
        k_ref[:, sl] = head(ksec[:, sl], kg_ref[...])
    vt = _dot_nt(wvt_ref[...], hn).astype(BF16)
    for g in range(KV_HEADS):
        vt_ref[g * VT_ROWS:g * VT_ROWS + HEAD_DIM, :] = vt[g * HEAD_DIM:(g + 1) * HEAD_DIM]
        vt_ref[g * VT_ROWS + HEAD_DIM:(g + 1) * VT_ROWS, :] = jnp.ones((VT_ROWS - HEAD_DIM, TM), BF16)
    c0 = ATTN_W + 2 * KV_W
    ga_ref[...] = _dot(hn, w_ref[:, c0:c0 + ATTN_W]).astype(BF16)
    c0 += ATTN_W
    u_ref[...] = _dot(hn, w_ref[:, c0:c0 + S5_W])
    c0 += S5_W
    gb_ref[...] = _dot(hn, w_ref[:, c0:c0 + S5_W]).astype(BF16)


def _even_in(layer, i, hres, gpre, w_in, wvt, qg, kg, cos, sin):
    lp = hres.shape[0]
    win_w = w_in.shape[-1]
    _full, _full_l = _slab(i), _slab(layer)
    row = lambda w: pl.BlockSpec((TM, w), lambda i: (i, 0))
    vt_rows = KV_HEADS * VT_ROWS
    col = pl.BlockSpec((vt_rows, TM), lambda i: (0, i))
    widths = [ATTN_W, KV_W, None, ATTN_W, S5_W, S5_W]
    return pl.pallas_call(
        _even_in_kernel,
        grid=(lp // TM,),
        in_specs=[row(D_MODEL), _full_l((1, D_MODEL)), _full((D_MODEL, win_w)), _full((KV_W, D_MODEL)),
                  _full((1, HEAD_DIM)), _full((1, HEAD_DIM)), row(HEAD_DIM), row(HEAD_DIM)],
        out_specs=[col if w is None else row(w) for w in widths],
        out_shape=[jax.ShapeDtypeStruct((vt_rows, lp) if w is None else (lp, w), F32 if n == 4 else BF16)
                   for n, w in enumerate(widths)],
        compiler_params=_cparams("parallel"),
        name="even_in",
    )(hres, gpre, w_in, wvt, qg, kg, cos, sin)


def _attn_kernel(q_ref, k_ref, vt_ref, o_ref, m_sc, acc_sc, p_sc, *, n_kv_tiles, stabilised):
    qs = [jnp.concatenate([q_ref[a * TQS:(a + 1) * TQS, r * HEAD_DIM:(r + 1) * HEAD_DIM] for r in range(KV_REP)],
                          axis=0) for a in range(TQ // TQS)]

    def scores(kb, a, mask_dummy=False):
        st = _dot_nt(kb, qs[a])
        if mask_dummy:
            key = lax.broadcasted_iota(jnp.int32, st.shape, 0)
            st = jnp.where(key >= N_DUMMY, st, NEG)
        return st

    def tile_start(j):
        return pl.multiple_of(FRONT + j * TK, 128)

    k0, vt0 = k_ref[0:FRONT, :], vt_ref[:, 0:FRONT]
    if stabilised:
        def update(kb, vtb, mask_dummy):
            for a in range(len(qs)):
                st = scores(kb, a, mask_dummy)
                m_old = m_sc[a]
                m_new = jnp.maximum(m_old, jnp.max(st, axis=0, keepdims=True))
                p = jnp.exp2(st - m_new).astype(BF16)
                acc_sc[a] = jnp.exp2(m_old - m_new) * acc_sc[a] + _dot(vtb, p)
                m_sc[a] = m_new

        m_sc[...] = jnp.full(m_sc.shape, NEG, F32)
        acc_sc[...] = jnp.zeros(acc_sc.shape, F32)
        update(k0, vt0, True)

        def body(j, carry):
            update(k_ref[pl.ds(tile_start(j), TK), :], vt_ref[:, pl.ds(tile_start(j), TK)], False)
            return carry

        lax.fori_loop(0, n_kv_tiles, body, 0)
    else:
        for a in range(len(qs)):
            acc_sc[a] = _dot(vt0, jnp.exp2(scores(k0, a, True).astype(BF16)))
            p_sc[0, a] = jnp.exp2(scores(k_ref[FRONT:FRONT + TK, :], a).astype(BF16))

        def body(j, carry):
            cur, prev = j % 2, (j + 1) % 2
            kb = k_ref[pl.ds(tile_start(j), TK), :]
            vtb = vt_ref[:, pl.ds(tile_start(j - 1), TK)]
            for a in range(len(qs)):
                acc_sc[a] += _dot(vtb, p_sc[prev, a])
            for a in range(len(qs)):
                p_sc[cur, a] = jnp.exp2(scores(kb, a).astype(BF16))
            return carry

        lax.fori_loop(1, n_kv_tiles, body, 0)
        last = (n_kv_tiles - 1) % 2
        vtb = vt_ref[:, FRONT + (n_kv_tiles - 1) * TK:FRONT + n_kv_tiles * TK]
        for a in range(len(qs)):
            acc_sc[a] += _dot(vtb, p_sc[last, a])

    for a in range(TQ // TQS):
        ot = acc_sc[a, 0:HEAD_DIM, :] / acc_sc[a, HEAD_DIM:HEAD_DIM + 1, :]
        for r in range(KV_REP):
            o_ref[a * TQS:(a + 1) * TQS, r * HEAD_DIM:(r + 1) * HEAD_DIM] = (
                ot[:, r * TQS:(r + 1) * TQS].T.astype(BF16))


def _attention(q, k, vt, stabilised):
    lp = q.shape[0]
    n_kv_tiles = (lp - FRONT) // TK
    gw = KV_REP * HEAD_DIM
    n_sub = TQ // TQS
    return pl.pallas_call(
        functools.partial(_attn_kernel, n_kv_tiles=n_kv_tiles, stabilised=stabilised),
        grid=(KV_HEADS, lp // TQ),
        in_specs=[pl.BlockSpec((TQ, gw), lambda g, i: (i, g)),
                  pl.BlockSpec((lp, HEAD_DIM), lambda g, i: (0, g)),
                  pl.BlockSpec((VT_ROWS, lp), lambda g, i: (g, 0))],
        out_specs=pl.BlockSpec((TQ, gw), lambda g, i: (i, g)),
        out_shape=jax.ShapeDtypeStruct((lp, ATTN_W), BF16),
        scratch_shapes=[pltpu.VMEM((n_sub, 1, KV_REP * TQS), F32),
                        pltpu.VMEM((n_sub, VT_ROWS, KV_REP * TQS), F32),
                        pltpu.VMEM((2, n_sub, TK, KV_REP * TQS), BF16)],
        compiler_params=_cparams("parallel", "parallel"),
        name="attention_stabilised" if stabilised else "attention",
    )(q, k, vt)


def _s5_tables(a_re, a_im, log_dt, b_re, b_im, c_re, c_im, d_skip):
    t = S5_T
    lag = jnp.arange(t + 1, dtype=F32)
    per_dir = []
    for d in range(2):
        a = lax.complex(a_re[d], a_im[d])
        adt = a * jnp.exp(log_dt[d])[:, None]
        a_bar = jnp.exp(adt)
        b_bar = ((a_bar - 1.0) / a)[..., None] * lax.complex(b_re[d], b_im[d])
        c = lax.complex(c_re[d], c_im[d])
        pw = jnp.exp(adt[None] * lag[:, None, None].astype(jnp.complex64))
        klag = jnp.real(jnp.einsum('gon,dgn,gni->dgoi', c, pw[:t], b_bar))
        per_dir.append((b_bar, c, pw, klag))
    (bb0, c0, pw0, kl0), (bb1, c1, pw1, kl1) = per_dir
    ii = jnp.arange(t)[:, None, None]
    jj = jnp.arange(t)[None, :, None]
    dd = jnp.arange(t)[None, None, :]
    place_f = (jj - ii == dd).astype(F32)
    place_b = (ii - jj == dd).astype(F32)
    skip = d_skip.reshape(S5_G, S5_P)[:, :, None] * jnp.eye(S5_P, dtype=F32)
    kl0 = kl0.at[0].add(skip)
    kmat = (jnp.einsum('ijd,dgop->gipjo', place_f, kl0) + jnp.einsum('ijd,dgop->gipjo', place_b, kl1)
            ).reshape(S5_G, S5_CW, S5_CW)

    bf = pw0[:t][::-1][..., None] * bb0[None]
    bb = pw1[:t][..., None] * bb1[None]
    cf = c0[None] * pw0[1:t + 1][:, :, None, :]
    cb = c1[None] * pw1[1:t + 1][::-1][:, :, None, :]

    def in_mat(x):
        return x.astype(BF16).transpose(1, 0, 3, 2).reshape(S5_G, S5_CW, S5_N)

    def out_mat(x):
        return x.astype(BF16).transpose(1, 3, 0, 2).reshape(S5_G, S5_N, S5_CW)

    n_pair = S5_G // 2
    zero_in = jnp.zeros((n_pair, S5_CW, S5_N), BF16)
    zero_out = jnp.zeros((n_pair, S5_N, S5_CW), BF16)

    def pair_in(x):
        e, o = x[0::2], x[1::2]
        return jnp.concatenate([jnp.concatenate([e, zero_in], axis=2),
                                jnp.concatenate([zero_in, o], axis=2)], axis=1)

    def pair_out(x):
        e, o = x[0::2], x[1::2]
        return jnp.concatenate([jnp.concatenate([e, zero_out], axis=2),
                                jnp.concatenate([zero_out, o], axis=2)], axis=1)

    ins = [in_mat(jnp.real(bf)), in_mat(jnp.imag(bf)), in_mat(jnp.real(bb)), in_mat(jnp.imag(bb))]
    outs = [out_mat(jnp.real(cf)), out_mat(-jnp.imag(cf)), out_mat(jnp.real(cb)), out_mat(-jnp.imag(cb))]
    ins, outs = lax.optimization_barrier((ins, outs))
    bmat = jnp.concatenate([pair_in(m) for m in ins], axis=2)
    cmat = jnp.concatenate([pair_out(m) for m in outs], axis=1)

    def pair_row(x):
        return x.reshape(S5_G // S5_GB, 1, S5_PB * 2 * S5_N)

    a_pow = jnp.concatenate([pair_row(jnp.real(pw0[t])), pair_row(jnp.imag(pw0[t])),
                             pair_row(jnp.real(pw1[t])), pair_row(jnp.imag(pw1[t]))], axis=1)
    return kmat.astype(BF16), bmat, cmat, a_pow


LANE_GROUPS = 128 // S5_P


def _swap_token_group(xs):
    block = lax.broadcasted_iota(jnp.int32, xs[0].shape, 1) // S5_P
    xs = list(xs)
    d = LANE_GROUPS // 2
    while d:
        upper = (block & d) != 0
        for a in range(LANE_GROUPS):
            if a & d:
                continue
            lo, hi = xs[a], xs[a + d]
            xs[a] = jnp.where(upper, pltpu.roll(hi, S5_P * d, 1), lo)
            xs[a + d] = jnp.where(upper, hi, pltpu.roll(lo, 128 - S5_P * d, 1))
        d //= 2
    return xs


def _s5_pack_kernel(u_ref, o_ref, *, n_chunks):
    n_main = n_chunks // 16 * 16
    for half in range(S5_T // LANE_GROUPS):
        xs = [u_ref[pl.ds(LANE_GROUPS * half + i, n_chunks, stride=S5_T), :] for i in range(LANE_GROUPS)]
        for g, y in enumerate(_swap_token_group(xs)):
            cols = slice(g * S5_CW + half * 128, g * S5_CW + (half + 1) * 128)
            o_ref[0:n_main, cols] = y[0:n_main].astype(BF16)
            if o_ref.shape[0] > n_main:
                tail = jnp.concatenate([y[n_main:], jnp.zeros((o_ref.shape[0] - n_chunks, 128), F32)], axis=0)
                o_ref[n_main:, cols] = tail.astype(BF16)


def _s5_unpack_kernel(y_ref, o_ref, *, n_chunks):
    for half in range(S5_T // LANE_GROUPS):
        ys = [y_ref[:, g * S5_CW + half * 128:g * S5_CW + (half + 1) * 128].astype(F32)[0:n_chunks]
              for g in range(LANE_GROUPS)]
        for i, x in enumerate(_swap_token_group(ys)):
            o_ref[pl.ds(LANE_GROUPS * half + i, n_chunks, stride=S5_T), :] = x


def _s5_pack(u, n_chunks_pad):
    lp = u.shape[0]
    bw = LANE_GROUPS * S5_CW
    return pl.pallas_call(
        functools.partial(_s5_pack_kernel, n_chunks=lp // S5_T),
        grid=(S5_W // 128,),
        in_specs=[pl.BlockSpec((lp, 128), lambda i: (0, i))],
        out_specs=pl.BlockSpec((n_chunks_pad, bw), lambda i: (0, i)),
        out_shape=jax.ShapeDtypeStruct((n_chunks_pad, S5_G * S5_CW), BF16),
        compiler_params=_cparams("parallel"),
        name="s5_pack",
    )(u)


def _s5_unpack(yc, lp):
    bw = LANE_GROUPS * S5_CW
    return pl.pallas_call(
        functools.partial(_s5_unpack_kernel, n_chunks=lp // S5_T),
        grid=(S5_W // 128,),
        in_specs=[pl.BlockSpec((yc.shape[0], bw), lambda i: (0, i))],
        out_specs=pl.BlockSpec((lp, 128), lambda i: (0, i)),
        out_shape=jax.ShapeDtypeStruct((lp, S5_W), F32),
        compiler_params=_cparams("parallel"),
        name="s5_unpack",
    )(yc)


def _s5_kernel(u_ref, k_ref, b_ref, c_ref, a_ref, y_ref, fre, fim, bre, bim, *, n_chunks):
    pw = 2 * S5_N
    for p in range(S5_PB):
        s = _dot(u_ref[:, p * 2 * S5_CW:(p + 1) * 2 * S5_CW], b_ref[p])
        cols = slice(p * pw, (p + 1) * pw)
        fre[:, cols] = s[:, 0:pw]
        fim[:, cols] = s[:, pw:2 * pw]
        bre[:, cols] = s[:, 2 * pw:3 * pw]
        bim[:, cols] = s[:, 3 * pw:4 * pw]

    far, fai, bar, bai = a_ref[0, 0:1, :], a_ref[0, 1:2, :], a_ref[0, 2:3, :], a_ref[0, 3:4, :]

    def step(i, carry):
        xr, xi, yr, yi = carry
        rf = pl.ds(i, 1)
        rb = pl.ds(n_chunks - 1 - i, 1)
        sr, si = fre[rf, :], fim[rf, :]
        tr, ti = bre[rb, :], bim[rb, :]
        fre[rf, :] = xr
        fim[rf, :] = xi
        bre[rb, :] = yr
        bim[rb, :] = yi
        return (far * xr - fai * xi + sr, far * xi + fai * xr + si,
                bar * yr - bai * yi + tr, bar * yi + bai * yr + ti)

    zero = jnp.zeros((1, S5_PB * pw), F32)
    lax.fori_loop(0, n_chunks, step, (zero, zero, zero, zero), unroll=S5_UNROLL)

    for p in range(S5_PB):
        cols = slice(p * pw, (p + 1) * pw)
        state = jnp.concatenate([fre[:, cols], fim[:, cols], bre[:, cols], bim[:, cols]], axis=1).astype(BF16)
        carry_out = _dot(state, c_ref[p])
        for e in range(2):
            g = 2 * p + e
            gc = slice(g * S5_CW, (g + 1) * S5_CW)
            y = _dot(u_ref[:, gc], k_ref[g]) + carry_out[:, e * S5_CW:(e + 1) * S5_CW]
            y_ref[:, gc] = y.astype(BF16)


def _s5(layer, u_chunks, kmat, bmat, cmat, a_pow):
    n_chunks = u_chunks.shape[0]
    bw = S5_GB * S5_CW
    sw = S5_PB * 2 * S5_N
    return pl.pallas_call(
        functools.partial(_s5_kernel, n_chunks=n_chunks),
        grid=(S5_G // S5_GB,),
        in_specs=[pl.BlockSpec((n_chunks, bw), lambda i: (0, i)),
                  pl.BlockSpec((None, S5_GB, S5_CW, S5_CW), lambda i: (layer, i, 0, 0)),
                  pl.BlockSpec((None, S5_PB, 2 * S5_CW, 8 * S5_N), lambda i: (layer, i, 0, 0)),
                  pl.BlockSpec((None, S5_PB, 8 * S5_N, 2 * S5_CW), lambda i: (layer, i, 0, 0)),
                  pl.BlockSpec((None, 1, 4, sw), lambda i: (layer, i, 0, 0))],
        out_specs=pl.BlockSpec((n_chunks, bw), lambda i: (0, i)),
        out_shape=jax.ShapeDtypeStruct(u_chunks.shape, BF16),
        scratch_shapes=[pltpu.VMEM((n_chunks, sw), F32)] * 4,
        compiler_params=_cparams("parallel"),
        name="s5_scan",
    )(u_chunks, kmat, bmat, cmat, a_pow)


def _row_keep(shape, tile_rows):
    row = pl.program_id(0) * tile_rows + lax.broadcasted_iota(jnp.int32, shape, 0)
    return row >= N_DUMMY


def _even_out_kernel(o_ref, ga_ref, y_ref, gb_ref, h_ref, wglu_ref, bglu_ref, wout_ref, gpost_ref, hout_ref):
    ya = o_ref[...].astype(F32) * _silu(ga_ref[...].astype(F32))
    yb = _gelu_tanh(y_ref[...].astype(F32))
    yb = yb * _sigmoid(_dot(yb.astype(BF16), wglu_ref[...]) + bglu_ref[...])
    yb = yb * _silu(gb_ref[...].astype(F32))
    out = _dot(ya.astype(BF16), wout_ref[0:ATTN_W, :]) + _dot(yb.astype(BF16), wout_ref[ATTN_W:, :])
    hnew = h_ref[...] + _rms(out, gpost_ref[...])
    hout_ref[...] = jnp.where(_row_keep(hnew.shape, TM), hnew, 0.0)


def _even_out(layer, i, o, ga, y, gb, hres, wglu, bglu, wout, gpost):
    lp = hres.shape[0]
    row = lambda w: pl.BlockSpec((TM, w), lambda i: (i, 0))
    _full, _full_l = _slab(i), _slab(layer)
    return pl.pallas_call(
        _even_out_kernel,
        grid=(lp // TM,),
        in_specs=[row(ATTN_W), row(ATTN_W), row(S5_W), row(S5_W), row(D_MODEL),
                  _full((S5_W, S5_W)), _full((1, S5_W)), _full((ATTN_W + S5_W, D_MODEL)), _full_l((1, D_MODEL))],
        out_specs=row(D_MODEL),
        out_shape=jax.ShapeDtypeStruct((lp, D_MODEL), F32),
        compiler_params=_cparams("parallel"),
        name="even_out",
    )(o, ga, y, gb, hres, wglu, bglu, wout, gpost)


def _odd_in_kernel(h_ref, gpre_ref, w_ref, xm_ref, z_ref):
    hn = _rms(h_ref[...], gpre_ref[...]).astype(BF16)
    xm_ref[...] = _dot(hn, w_ref[:, 0:ML_W]).astype(BF16)
    z_ref[...] = _dot(hn, w_ref[:, ML_W:]).astype(BF16)


def _odd_in(layer, i, hres, gpre, w_in):
    lp = hres.shape[0]
    row = lambda w: pl.BlockSpec((TM, w), lambda i: (i, 0))
    return pl.pallas_call(
        _odd_in_kernel,
        grid=(lp // TM,),
        in_specs=[row(D_MODEL), _slab(layer)((1, D_MODEL)), _slab(i)((D_MODEL, 2 * ML_W))],
        out_specs=[row(ML_W), row(ML_W)],
        out_shape=[jax.ShapeDtypeStruct((lp, ML_W), BF16)] * 2,
        compiler_params=_cparams("parallel"),
        name="odd_in",
    )(hres, gpre, w_in)


HALO = 16
CONV_BLOCK = 128
GATE_LANES = 128


def _conv_shift_matrix():
    offs = [j - CONV_W // 2 for j in range(CONV_W) if j != CONV_W // 2]
    r = jnp.arange(CONV_BLOCK)[None, :, None]
    c = jnp.arange(CONV_BLOCK + 2 * HALO)[None, None, :]
    off = jnp.asarray(offs)[:, None, None]
    return (c == r + HALO + off).astype(BF16).reshape(len(offs) * CONV_BLOCK, CONV_BLOCK + 2 * HALO)


def _odd_mid_kernel(xm_ref, prev_ref, next_ref, sh_ref, cw_ref, cb_ref, wqk_ref, wv_ref, wg_ref, bg_ref,
                    q_ref, k_ref, v_ref, xc_ref, g_ref, ext):
    i = pl.program_id(0)
    n = pl.num_programs(0)
    xm = xm_ref[...]
    ext[0:HALO, :] = jnp.where(i > 0, prev_ref[...], jnp.zeros((), BF16))
    ext[HALO:HALO + TM, :] = xm
    ext[HALO + TM:, :] = jnp.where(i < n - 1, next_ref[...], jnp.zeros((), BF16))
    centre = CONV_W // 2
    for b in range(TM // CONV_BLOCK):
        rows = slice(b * CONV_BLOCK, (b + 1) * CONV_BLOCK)
        shifted = _dot(sh_ref[...], ext[b * CONV_BLOCK:(b + 1) * CONV_BLOCK + 2 * HALO, :])
        conv = cb_ref[...] + cw_ref[centre:centre + 1, :] * xm[rows].astype(F32)
        for n_tap, j in enumerate(j for j in range(CONV_W) if j != centre):
            conv = conv + cw_ref[j:j + 1, :] * shifted[n_tap * CONV_BLOCK:(n_tap + 1) * CONV_BLOCK]
        xc_ref[rows, :] = _silu(conv).astype(BF16)
    xcb = xc_ref[...]

    for h in range(ML_H):
        cin = slice(h * ML_DV, (h + 1) * ML_DV)
        qk = _dot(xcb[:, cin], wqk_ref[h])
        q_ref[:, h * ML_DK:(h + 1) * ML_DK] = (qk[:, 0:ML_DK] * (ML_DK ** -0.5)).astype(BF16)
        k_ref[:, h * ML_DK:(h + 1) * ML_DK] = qk[:, ML_DK:].astype(BF16)
        v_ref[:, cin] = _dot(xm[:, cin], wv_ref[h]).astype(BF16)

    pre = _dot(xcb, wg_ref[...]) + bg_ref[...]
    lane = lax.broadcasted_iota(jnp.int32, pre.shape, 1)
    keep = _row_keep(pre.shape, TM)
    li = jnp.where(keep, pre, NEG_BIG)
    lf = jnp.where(keep, jnp.minimum(pre, 0.0) - jnp.log(1.0 + jnp.exp(-jnp.abs(pre))), 0.0)
    t = lax.broadcasted_iota(jnp.int32, (TM, TM), 0)
    s = lax.broadcasted_iota(jnp.int32, (TM, TM), 1)
    same = (t // ML_CHUNK) == (s // ML_CHUNK)
    tri_f = jnp.where(same & (s <= t), 1.0, 0.0).astype(BF16)
    tri_b = jnp.where(same & (s >= t), 1.0, 0.0).astype(BF16)
    lf_hi = lf.astype(BF16)
    lf_lo = (lf - lf_hi.astype(F32)).astype(BF16)
    cum_f = _dot(tri_f, lf_hi) + _dot(tri_f, lf_lo)
    cum_b = _dot(tri_b, lf_hi) + _dot(tri_b, lf_lo)
    g_ref[...] = jnp.where(lane < 2 * ML_H, li, jnp.where(lane < 3 * ML_H, cum_f, cum_b))


def _odd_mid(i, xm, cw, cb, wqk, wv, wg, bg):
    lp = xm.shape[0]
    row = lambda w: pl.BlockSpec((TM, w), lambda i: (i, 0))
    _full = _slab(i)
    per = TM // HALO
    last = lp // HALO - 1
    shift = _conv_shift_matrix()
    return pl.pallas_call(
        _odd_mid_kernel,
        grid=(lp // TM,),
        in_specs=[row(ML_W),
                  pl.BlockSpec((HALO, ML_W), lambda i: (jnp.maximum(i * per - 1, 0), 0)),
                  pl.BlockSpec((HALO, ML_W), lambda i: (jnp.minimum((i + 1) * per, last), 0)),
                  pl.BlockSpec(shift.shape, lambda i: (0, 0)),
                  _full((CONV_W, ML_W)), _full((1, ML_W)),
                  _full((ML_H, ML_DV, 2 * ML_DK)), _full((ML_H, ML_DV, ML_DV)),
                  _full((ML_W, GATE_LANES)), _full((1, GATE_LANES))],
        out_specs=[row(ML_H * ML_DK), row(ML_H * ML_DK), row(ML_W), row(ML_W), row(GATE_LANES)],
        out_shape=[jax.ShapeDtypeStruct((lp, ML_H * ML_DK), BF16), jax.ShapeDtypeStruct((lp, ML_H * ML_DK), BF16),
                   jax.ShapeDtypeStruct((lp, ML_W), BF16), jax.ShapeDtypeStruct((lp, ML_W), BF16),
                   jax.ShapeDtypeStruct((lp, GATE_LANES), F32)],
        scratch_shapes=[pltpu.VMEM((TM + 2 * HALO, ML_W), BF16)],
        compiler_params=_cparams("parallel"),
        name="odd_mid",
    )(xm, xm, xm, shift, cw, cb, wqk, wv, wg, bg)


def _mlstm_chunk(q, k, v, g, gt, h, backward, state, m_prev):
    c = ML_CHUNK
    off = ML_H if backward else 0
    li_row = gt[off + h:off + h + 1, :]
    b_row = gt[2 * ML_H + off + h:2 * ML_H + off + h + 1, :]
    a_row = li_row - b_row
    a_col = g[:, off + h:off + h + 1] - g[:, 2 * ML_H + off + h:2 * ML_H + off + h + 1]
    b_last = jnp.broadcast_to(b_row[:, 0:1] if backward else b_row[:, c - 1:c], (1, c))

    kq = _dot_nt(k, q)
    vt_aug = jnp.concatenate([v.T, jnp.ones((ML_AUG, c), BF16)], axis=0)
    qt = q.T.astype(F32)
    m_new = b_last + jnp.maximum(m_prev, jnp.max(a_row, axis=-1, keepdims=True))
    decay = jnp.exp(b_last + m_prev - m_new)
    w_row = jnp.exp(b_last + a_row - m_new)
    update = _dot((vt_aug.astype(F32) * w_row).astype(BF16), k)
    yield decay * state + update, m_new

    s = lax.broadcasted_iota(jnp.int32, (c, c), 0)
    t = lax.broadcasted_iota(jnp.int32, (c, c), 1)
    allowed = (s >= t) if backward else (s <= t)
    dt = jnp.where(allowed, b_row + a_col, NEG)
    inter_log = b_row + m_prev
    m_t = jnp.maximum(jnp.max(dt, axis=0, keepdims=True), inter_log)
    swt = jnp.exp(dt - m_t) * kq
    inter = jnp.exp(inter_log - m_t)
    num = _dot(jnp.concatenate([vt_aug, state.astype(BF16)], axis=1),
               jnp.concatenate([swt.astype(BF16), (qt * inter).astype(BF16)], axis=0))
    yield None

    den = num[ML_DV:ML_DV + 1, :]
    out_t = num[0:ML_DV, :] / jnp.maximum(jnp.abs(den), jnp.exp(-m_t))
    yield out_t.T


def _mlstm_kernel(qf_ref, kf_ref, vf_ref, gf_ref, qb_ref, kb_ref, vb_ref, gb_ref, hf_ref, hb_ref, c_sc, m_sc):
    @pl.when(pl.program_id(0) == 0)
    def _():
        c_sc[...] = jnp.zeros(c_sc.shape, F32)
        m_sc[...] = jnp.zeros(m_sc.shape, F32)

    chunks = []
    for backward, (q_ref, k_ref, v_ref, g_ref, h_ref) in enumerate(
            ((qf_ref, kf_ref, vf_ref, gf_ref, hf_ref), (qb_ref, kb_ref, vb_ref, gb_ref, hb_ref))):
        order = range(ML_STEP_CHUNKS - 1, -1, -1) if backward else range(ML_STEP_CHUNKS)
        gs = [g_ref[c * ML_CHUNK:(c + 1) * ML_CHUNK, :] for c in range(ML_STEP_CHUNKS)]
        gts = [g.T for g in gs]
        for h in range(ML_H):
            slot = backward * ML_H + h
            state, m = c_sc[slot], m_sc[slot]
            for c in order:
                rows = slice(c * ML_CHUNK, (c + 1) * ML_CHUNK)
                gen = _mlstm_chunk(q_ref[rows, h * ML_DK:(h + 1) * ML_DK], k_ref[rows, h * ML_DK:(h + 1) * ML_DK],
                                   v_ref[rows, h * ML_DV:(h + 1) * ML_DV], gs[c], gts[c], h, bool(backward),
                                   state, m)
                state, m = next(gen)
                chunks.append((gen, h_ref, rows, h))
            c_sc[slot] = state
            m_sc[slot] = m
    for gen, _, _, _ in chunks:
        next(gen)
    for gen, h_ref, rows, h in chunks:
        h_ref[rows, h * ML_DV:(h + 1) * ML_DV] = next(gen).astype(BF16)


def _mlstm(q, k, v, gates):
    lp = q.shape[0]
    rows = ML_STEP_CHUNKS * ML_CHUNK
    nch = lp // rows
    fwd = lambda w: pl.BlockSpec((rows, w), lambda i: (i, 0))
    bwd = lambda w: pl.BlockSpec((rows, w), lambda i: (nch - 1 - i, 0))
    qk_w = ML_H * ML_DK
    return pl.pallas_call(
        _mlstm_kernel,
        grid=(nch,),
        in_specs=[fwd(qk_w), fwd(qk_w), fwd(ML_W), fwd(GATE_LANES),
                  bwd(qk_w), bwd(qk_w), bwd(ML_W), bwd(GATE_LANES)],
        out_specs=[fwd(ML_W), bwd(ML_W)],
        out_shape=[jax.ShapeDtypeStruct((lp, ML_W), BF16)] * 2,
        scratch_shapes=[pltpu.VMEM((2 * ML_H, ML_DV + ML_AUG, ML_DK), F32),
                        pltpu.VMEM((2 * ML_H, 1, ML_CHUNK), F32)],
        compiler_params=_cparams("arbitrary"),
        name="mlstm",
    )(q, k, v, gates, q, k, v, gates)


def _odd_out_kernel(hf_ref, hb_ref, xc_ref, z_ref, h_ref, wo_ref, bo_ref, hn_ref, sk_ref, wout_ref, gpost_ref,
                    hout_ref):
    xcb = xc_ref[...]
    xc = xcb.astype(F32)
    og = _sigmoid(_dot(xcb, wo_ref[...]) + bo_ref[...])
    cell = hf_ref[...].astype(F32) + hb_ref[...].astype(F32)
    parts = []
    for h in range(ML_H):
        sl = slice(h * ML_DV, (h + 1) * ML_DV)
        parts.append(_rms(cell[:, sl], hn_ref[:, sl]))
    cell = jnp.concatenate(parts, axis=1)
    hh = (og * cell + sk_ref[...] * xc) * _silu(z_ref[...].astype(F32))
    out = _dot(hh.astype(BF16), wout_ref[...])
    hnew = h_ref[...] + _rms(out, gpost_ref[...])
    hout_ref[...] = jnp.where(_row_keep(hnew.shape, TM), hnew, 0.0)


def _odd_out(layer, i, hf, hb, xc, z, hres, wo, bo, hn, sk, wout, gpost):
    lp = hres.shape[0]
    row = lambda w: pl.BlockSpec((TM, w), lambda i: (i, 0))
    _full, _full_l = _slab(i), _slab(layer)
    return pl.pallas_call(
        _odd_out_kernel,
        grid=(lp // TM,),
        in_specs=[row(ML_W), row(ML_W), row(ML_W), row(ML_W), row(D_MODEL),
                  _full((ML_W, ML_W)), _full((1, ML_W)), _full((1, ML_W)), _full((1, ML_W)),
                  _full((ML_W, D_MODEL)), _full_l((1, D_MODEL))],
        out_specs=row(D_MODEL),
        out_shape=jax.ShapeDtypeStruct((lp, D_MODEL), F32),
        compiler_params=_cparams("parallel"),
        name="odd_out",
    )(hf, hb, xc, z, hres, wo, bo, hn, sk, wout, gpost)


def _rope_tables(n_real):
    rows = n_real // GRID_W
    freqs = ROPE_THETA ** (-jnp.arange(0, ROPE_AXIS, 2, dtype=F32) / ROPE_AXIS)
    half = freqs.shape[0]
    ang_r = jnp.arange(rows, dtype=F32)[:, None] * freqs[None]
    ang_c = jnp.arange(GRID_W, dtype=F32)[:, None] * freqs[None]
    per_row = lambda t: jnp.broadcast_to(t[:, None, :], (rows, GRID_W, half)).reshape(n_real, half)
    per_col = lambda t: jnp.broadcast_to(t[None, :, :], (rows, GRID_W, half)).reshape(n_real, half)
    cr, sr = per_row(jnp.cos(ang_r)), per_row(jnp.sin(ang_r))
    cc, sc = per_col(jnp.cos(ang_c)), per_col(jnp.sin(ang_c))
    cos = jnp.concatenate([cr, cr, cc, cc], axis=1)
    sin = jnp.concatenate([-sr, sr, -sc, sc], axis=1)
    cos = jnp.concatenate([jnp.ones((FRONT, HEAD_DIM), F32), cos], axis=0)
    sin = jnp.concatenate([jnp.zeros((FRONT, HEAD_DIM), F32), sin], axis=0)
    return cos, sin


def kernel(x, meta_tokens, norm_pre, norm_post, w_in_even, q_norm, k_norm, ssm_a_re, ssm_a_im, ssm_log_dt, ssm_b_re, ssm_b_im, ssm_c_re, ssm_c_im, ssm_d, w_glu, b_glu, w_out_even, w_in_odd, conv_w, conv_b, w_q, w_k, w_v, w_igate, b_igate, w_fgate, b_fgate, w_ogate, b_ogate, head_norm, mlstm_skip, w_out_odd):
    bsz, n_real, _ = x.shape
    assert bsz == 1 and n_real % TK == 0 and (n_real + FRONT) % TM == 0
    lp = n_real + FRONT
    depth = norm_pre.shape[0]
    hres = jnp.concatenate([jnp.zeros((N_DUMMY, D_MODEL), F32), meta_tokens.astype(F32), x[0]], axis=0)
    cos, sin = _rope_tables(n_real)
    n_chunks_pad = -(-(lp // S5_T) // 16) * 16

    bf = lambda a: a.astype(BF16)
    vec = lambda a: a.reshape(a.shape[0], 1, -1)
    norm_pre, norm_post = vec(norm_pre), vec(norm_post)
    w_in_even = bf(w_in_even)
    wvt = jnp.swapaxes(w_in_even[:, :, ATTN_W + KV_W:ATTN_W + 2 * KV_W], 1, 2)
    qg = vec(q_norm) * (HEAD_DIM ** -0.5 * math.log2(math.e))
    kmat, bmat, cmat, a_pow = jax.vmap(_s5_tables)(ssm_a_re, ssm_a_im, ssm_log_dt, ssm_b_re, ssm_b_im,
                                                   ssm_c_re, ssm_c_im, ssm_d)
    w_glu, w_out_even = bf(w_glu), bf(w_out_even)
    w_in_odd, w_ogate, w_out_odd, w_v = bf(w_in_odd), bf(w_ogate), bf(w_out_odd), bf(w_v)
    wqk = bf(jnp.concatenate([w_q, w_k], axis=3))
    gate_pad = GATE_LANES - 4 * ML_H
    wg = jnp.concatenate([w_igate[:, 0], w_igate[:, 1], w_fgate[:, 0], w_fgate[:, 1]], axis=2)
    wg = bf(jnp.pad(wg, ((0, 0), (0, 0), (0, gate_pad))))
    bg = jnp.concatenate([b_igate[:, 0], b_igate[:, 1], b_fgate[:, 0], b_fgate[:, 1]], axis=1)
    bg = vec(jnp.pad(bg, ((0, 0), (0, gate_pad))))

    for layer in range(depth):
        i = layer // 2
        if layer % 2 == 0:
            q, k, vt, ga, u, gb = _even_in(layer, i, hres, norm_pre, w_in_even, wvt, qg, vec(k_norm), cos, sin)
            logit_bound = (math.sqrt(HEAD_DIM) * math.log2(math.e)
                           * jnp.max(jnp.abs(q_norm[i])) * jnp.max(jnp.abs(k_norm[i])))
            o = lax.cond(logit_bound <= LOGIT_SAFE,
                         lambda q, k, vt: _attention(q, k, vt, False),
                         lambda q, k, vt: _attention(q, k, vt, True), q, k, vt)
            y = _s5_unpack(_s5(i, _s5_pack(u, n_chunks_pad), kmat, bmat, cmat, a_pow), lp)
            hres = _even_out(layer, i, o, ga, y, gb, hres, w_glu, vec(b_glu), w_out_even, norm_post)
        else:
            xm, z = _odd_in(layer, i, hres, norm_pre, w_in_odd)
            qm, km, vm, xc, gates = _odd_mid(i, xm, conv_w, vec(conv_b), wqk, w_v, wg, bg)
            hf, hb = _mlstm(qm, km, vm, gates)
            hres = _odd_out(layer, i, hf, hb, xc, z, hres, w_ogate, vec(b_ogate), vec(head_norm), vec(mlstm_skip),
                            w_out_odd, norm_post)
    return hres[FRONT:].reshape(1, n_real, D_MODEL)
```

```python
import functools
import math

import jax
import jax.numpy as jnp
from jax import lax
from jax.experimental import pallas as pl
from jax.experimental.pallas import tpu as pltpu

F32 = jnp.float32
BF16 = jnp.bfloat16

D_MODEL = 1024
N_META = 16
FRONT = 128
N_DUMMY = FRONT - N_META
GRID_W = 64
EPS = 1e-6
NEG = -1e30

HEADS = 8
KV_HEADS = 2
KV_REP = HEADS // KV_HEADS
HEAD_DIM = 128
ROPE_AXIS = HEAD_DIM // 2
ROPE_THETA = 10000.0
ATTN_W = HEADS * HEAD_DIM
KV_W = KV_HEADS * HEAD_DIM
TQ = 384
TQS = 128
TK = 4096
VT_ROWS = HEAD_DIM + 16
LOGIT_SAFE = 60.0

S5_W = 1024
S5_P = 16
S5_G = S5_W // S5_P
S5_N = 64
S5_T = 8
S5_GB = 8
S5_PB = S5_GB // 2
S5_CW = S5_T * S5_P
S5_UNROLL = 8

ML_W = 2048
ML_H = 8
ML_DV = 256
ML_DK = 128
CONV_W = 5
ML_CHUNK = 128
ML_STEP_CHUNKS = 3
ML_AUG = 16
NEG_BIG = -1e9

TM = 384
VMEM_LIMIT = 56 * 1024 * 1024


def _cparams(*sem):
    return pltpu.CompilerParams(dimension_semantics=sem, vmem_limit_bytes=VMEM_LIMIT)


def _rms(x, g):
    return x * lax.rsqrt(jnp.mean(x * x, axis=-1, keepdims=True) + EPS) * g


def _sigmoid(x):
    return 0.5 * jnp.tanh(0.5 * x) + 0.5


def _silu(x):
    return x * _sigmoid(x)


def _gelu_tanh(x):
    return 0.5 * x * (1.0 + jnp.tanh(math.sqrt(2.0 / math.pi) * (x + 0.044715 * (x * x * x))))


def _dot(a, b):
    return jnp.dot(a, b, preferred_element_type=F32)


def _dot_nt(a, b):
    return lax.dot_general(a, b, (((1,), (1,)), ((), ())), preferred_element_type=F32)


def _dot_tn(a, b):
    return lax.dot_general(a, b, (((0,), (0,)), ((), ())), preferred_element_type=F32)


def _slab(layer):
    def spec(shape):
        return pl.BlockSpec((None,) + tuple(shape), lambda *_: (layer,) + (0,) * len(shape))
    return spec


def _even_in_kernel(h_ref, gpre_ref, w_ref, wvt_ref, qg_ref, kg_ref, cos_ref, sin_ref,
                    q_ref, k_ref, vt_ref, ga_ref, u_ref, gb_ref):
    hn = _rms(h_ref[...], gpre_ref[...]).astype(BF16)
    cos = cos_ref[...]
    sin = sin_ref[...]
    lane = lax.broadcasted_iota(jnp.int32, cos.shape, 1)
    first_half = (lane % ROPE_AXIS) < (ROPE_AXIS // 2)

    def head(x, g):
        y = _rms(x, g)
        partner = jnp.where(first_half, pltpu.roll(y, HEAD_DIM - ROPE_AXIS // 2, 1),
                            pltpu.roll(y, ROPE_AXIS // 2, 1))
        return (y * cos + partner * sin).astype(BF16)

    qsec = _dot(hn, w_ref[:, 0:ATTN_W])
    for h in range(HEADS):
        sl = slice(h * HEAD_DIM, (h + 1) * HEAD_DIM)
        q_ref[:, sl] = head(qsec[:, sl], qg_ref[...])
    ksec = _dot(hn, w_ref[:, ATTN_W:ATTN_W + KV_W])
    for h in range(KV_HEADS):
        sl = slice(h * HEAD_DIM, (h + 1) * HEAD_DIM)
        k_ref[:, sl] = head(ksec[:, sl], kg_ref[...])
    vt = _dot_nt(wvt_ref[...], hn).astype(BF16)
    for g in range(KV_HEADS):
        vt_ref[g * VT_ROWS:g * VT_ROWS + HEAD_DIM, :] = vt[g * HEAD_DIM:(g + 1) * HEAD_DIM]
        vt_ref[g * VT_ROWS + HEAD_DIM:(g + 1) * VT_ROWS, :] = jnp.ones((VT_ROWS - HEAD_DIM, TM), BF16)
    c0 = ATTN_W + 2 * KV_W
    ga_ref[...] = _dot(hn, w_ref[:, c0:c0 + ATTN_W]).astype(BF16)
    c0 += ATTN_W
    u_ref[...] = _dot(hn, w_ref[:, c0:c0 + S5_W])
    c0 += S5_W
    gb_ref[...] = _dot(hn, w_ref[:, c0:c0 + S5_W]).astype(BF16)


def _even_in(layer, i, hres, gpre, w_in, wvt, qg, kg, cos, sin):
    lp = hres.shape[0]
    win_w = w_in.shape[-1]
    _full, _full_l = _slab(i), _slab(layer)
    row = lambda w: pl.BlockSpec((TM, w), lambda i: (i, 0))
    vt_rows = KV_HEADS * VT_ROWS
    col = pl.BlockSpec((vt_rows, TM), lambda i: (0, i))
    widths = [ATTN_W, KV_W, None, ATTN_W, S5_W, S5_W]
    return pl.pallas_call(
        _even_in_kernel,
        grid=(lp // TM,),
        in_specs=[row(D_MODEL), _full_l((1, D_MODEL)), _full((D_MODEL, win_w)), _full((KV_W, D_MODEL)),
                  _full((1, HEAD_DIM)), _full((1, HEAD_DIM)), row(HEAD_DIM), row(HEAD_DIM)],
        out_specs=[col if w is None else row(w) for w in widths],
        out_shape=[jax.ShapeDtypeStruct((vt_rows, lp) if w is None else (lp, w), F32 if n == 4 else BF16)
                   for n, w in enumerate(widths)],
        compiler_params=_cparams("parallel"),
        name="even_in",
    )(hres, gpre, w_in, wvt, qg, kg, cos, sin)


def _attn_kernel(q_ref, k_ref, vt_ref, o_ref, m_sc, acc_sc, p_sc, *, n_kv_tiles, stabilised):
    qs = [jnp.concatenate([q_ref[a * TQS:(a + 1) * TQS, r * HEAD_DIM:(r + 1) * HEAD_DIM] for r in range(KV_REP)],
                          axis=0) for a in range(TQ // TQS)]

    def scores(kb, a, mask_dummy=False):
        st = _dot_nt(kb, qs[a])
        if mask_dummy:
            key = lax.broadcasted_iota(jnp.int32, st.shape, 0)
            st = jnp.where(key >= N_DUMMY, st, NEG)
        return st

    def tile_start(j):
        return pl.multiple_of(FRONT + j * TK, 128)

    k0, vt0 = k_ref[0:FRONT, :], vt_ref[:, 0:FRONT]
    if stabilised:
        def update(kb, vtb, mask_dummy):
            for a in range(len(qs)):
                st = scores(kb, a, mask_dummy)
                m_old = m_sc[a]
                m_new = jnp.maximum(m_old, jnp.max(st, axis=0, keepdims=True))
                p = jnp.exp2(st - m_new).astype(BF16)
                acc_sc[a] = jnp.exp2(m_old - m_new) * acc_sc[a] + _dot(vtb, p)
                m_sc[a] = m_new

        m_sc[...] = jnp.full(m_sc.shape, NEG, F32)
        acc_sc[...] = jnp.zeros(acc_sc.shape, F32)
        update(k0, vt0, True)

        def body(j, carry):
            update(k_ref[pl.ds(tile_start(j), TK), :], vt_ref[:, pl.ds(tile_start(j), TK)], False)
            return carry

        lax.fori_loop(0, n_kv_tiles, body, 0)
    else:
        for a in range(len(qs)):
            acc_sc[a] = _dot(vt0, jnp.exp2(scores(k0, a, True).astype(BF16)))
            p_sc[0, a] = jnp.exp2(scores(k_ref[FRONT:FRONT + TK, :], a).astype(BF16))

        def body(j, carry):
            cur, prev = j % 2, (j + 1) % 2
            kb = k_ref[pl.ds(tile_start(j), TK), :]
            vtb = vt_ref[:, pl.ds(tile_start(j - 1), TK)]
            for a in range(len(qs)):
                acc_sc[a] += _dot(vtb, p_sc[prev, a])
            for a in range(len(qs)):
                p_sc[cur, a] = jnp.exp2(scores(kb, a).astype(BF16))
            return carry

        lax.fori_loop(1, n_kv_tiles, body, 0)
        last = (n_kv_tiles - 1) % 2
        vtb = vt_ref[:, FRONT + (n_kv_tiles - 1) * TK:FRONT + n_kv_tiles * TK]
        for a in range(len(qs)):
            acc_sc[a] += _dot(vtb, p_sc[last, a])

    for a in range(TQ // TQS):
        ot = acc_sc[a, 0:HEAD_DIM, :] / acc_sc[a, HEAD_DIM:HEAD_DIM + 1, :]
        for r in range(KV_REP):
            o_ref[a * TQS:(a + 1) * TQS, r * HEAD_DIM:(r + 1) * HEAD_DIM] = (
                ot[:, r * TQS:(r + 1) * TQS].T.astype(BF16))


def _attention(q, k, vt, stabilised):
    lp = q.shape[0]
    n_kv_tiles = (lp - FRONT) // TK
    gw = KV_REP * HEAD_DIM
    n_sub = TQ // TQS
    return pl.pallas_call(
        functools.partial(_attn_kernel, n_kv_tiles=n_kv_tiles, stabilised=stabilised),
        grid=(KV_HEADS, lp // TQ),
        in_specs=[pl.BlockSpec((TQ, gw), lambda g, i: (i, g)),
                  pl.BlockSpec((lp, HEAD_DIM), lambda g, i: (0, g)),
                  pl.BlockSpec((VT_ROWS, lp), lambda g, i: (g, 0))],
        out_specs=pl.BlockSpec((TQ, gw), lambda g, i: (i, g)),
        out_shape=jax.ShapeDtypeStruct((lp, ATTN_W), BF16),
        scratch_shapes=[pltpu.VMEM((n_sub, 1, KV_REP * TQS), F32),
                        pltpu.VMEM((n_sub, VT_ROWS, KV_REP * TQS), F32),
                        pltpu.VMEM((2, n_sub, TK, KV_REP * TQS), BF16)],
        compiler_params=_cparams("parallel", "parallel"),
        name="attention_stabilised" if stabilised else "attention",
    )(q, k, vt)


def _s5_tables(a_re, a_im, log_dt, b_re, b_im, c_re, c_im, d_skip):
    t = S5_T
    lag = jnp.arange(t + 1, dtype=F32)
    per_dir = []
    for d in range(2):
        a = lax.complex(a_re[d], a_im[d])
        adt = a * jnp.exp(log_dt[d])[:, None]
        a_bar = jnp.exp(adt)
        b_bar = ((a_bar - 1.0) / a)[..., None] * lax.complex(b_re[d], b_im[d])
        c = lax.complex(c_re[d], c_im[d])
        pw = jnp.exp(adt[None] * lag[:, None, None].astype(jnp.complex64))
        klag = jnp.real(jnp.einsum('gon,dgn,gni->dgoi', c, pw[:t], b_bar))
        per_dir.append((b_bar, c, pw, klag))
    (bb0, c0, pw0, kl0), (bb1, c1, pw1, kl1) = per_dir
    ii = jnp.arange(t)[:, None, None]
    jj = jnp.arange(t)[None, :, None]
    dd = jnp.arange(t)[None, None, :]
    place_f = (jj - ii == dd).astype(F32)
    place_b = (ii - jj == dd).astype(F32)
    skip = d_skip.reshape(S5_G, S5_P)[:, :, None] * jnp.eye(S5_P, dtype=F32)
    kl0 = kl0.at[0].add(skip)
    kmat = (jnp.einsum('ijd,dgop->gipjo', place_f, kl0) + jnp.einsum('ijd,dgop->gipjo', place_b, kl1)
            ).reshape(S5_G, S5_CW, S5_CW)

    bf = pw0[:t][::-1][..., None] * bb0[None]
    bb = pw1[:t][..., None] * bb1[None]
    cf = c0[None] * pw0[1:t + 1][:, :, None, :]
    cb = c1[None] * pw1[1:t + 1][::-1][:, :, None, :]

    def in_mat(x):
        return x.astype(BF16).transpose(1, 0, 3, 2).reshape(S5_G, S5_CW, S5_N)

    def out_mat(x):
        return x.astype(BF16).transpose(1, 3, 0, 2).reshape(S5_G, S5_N, S5_CW)

    n_pair = S5_G // 2
    zero_in = jnp.zeros((n_pair, S5_CW, S5_N), BF16)
    zero_out = jnp.zeros((n_pair, S5_N, S5_CW), BF16)

    def pair_in(x):
        e, o = x[0::2], x[1::2]
        return jnp.concatenate([jnp.concatenate([e, zero_in], axis=2),
                                jnp.concatenate([zero_in, o], axis=2)], axis=1)

    def pair_out(x):
        e, o = x[0::2], x[1::2]
        return jnp.concatenate([jnp.concatenate([e, zero_out], axis=2),
                                jnp.concatenate([zero_out, o], axis=2)], axis=1)

    ins = [in_mat(jnp.real(bf)), in_mat(jnp.imag(bf)), in_mat(jnp.real(bb)), in_mat(jnp.imag(bb))]
    outs = [out_mat(jnp.real(cf)), out_mat(-jnp.imag(cf)), out_mat(jnp.real(cb)), out_mat(-jnp.imag(cb))]
    ins, outs = lax.optimization_barrier((ins, outs))
    bmat = jnp.concatenate([pair_in(m) for m in ins], axis=2)
    cmat = jnp.concatenate([pair_out(m) for m in outs], axis=1)

    def pair_row(x):
        return x.reshape(S5_G // S5_GB, 1, S5_PB * 2 * S5_N)

    a_pow = jnp.concatenate([pair_row(jnp.real(pw0[t])), pair_row(jnp.imag(pw0[t])),
                             pair_row(jnp.real(pw1[t])), pair_row(jnp.imag(pw1[t]))], axis=1)
    return kmat.astype(BF16), bmat, cmat, a_pow


LANE_GROUPS = 128 // S5_P


def _swap_token_group(xs):
    block = lax.broadcasted_iota(jnp.int32, xs[0].shape, 1) // S5_P
    xs = list(xs)
    d = LANE_GROUPS // 2
    while d:
        upper = (block & d) != 0
        for a in range(LANE_GROUPS):
            if a & d:
                continue
            lo, hi = xs[a], xs[a + d]
            xs[a] = jnp.where(upper, pltpu.roll(hi, S5_P * d, 1), lo)
            xs[a + d] = jnp.where(upper, hi, pltpu.roll(lo, 128 - S5_P * d, 1))
        d //= 2
    return xs


def _s5_pack_kernel(u_ref, o_ref, *, n_chunks):
    n_main = n_chunks // 16 * 16
    for half in range(S5_T // LANE_GROUPS):
        xs = [u_ref[pl.ds(LANE_GROUPS * half + i, n_chunks, stride=S5_T), :] for i in range(LANE_GROUPS)]
        for g, y in enumerate(_swap_token_group(xs)):
            cols = slice(g * S5_CW + half * 128, g * S5_CW + (half + 1) * 128)
            o_ref[0:n_main, cols] = y[0:n_main].astype(BF16)
            if o_ref.shape[0] > n_main:
                tail = jnp.concatenate([y[n_main:], jnp.zeros((o_ref.shape[0] - n_chunks, 128), F32)], axis=0)
                o_ref[n_main:, cols] = tail.astype(BF16)


def _s5_unpack_kernel(y_ref, o_ref, *, n_chunks):
    for half in range(S5_T // LANE_GROUPS):
        ys = [y_ref[:, g * S5_CW + half * 128:g * S5_CW + (half + 1) * 128].astype(F32)[0:n_chunks]
              for g in range(LANE_GROUPS)]
        for i, x in enumerate(_swap_token_group(ys)):
            o_ref[pl.ds(LANE_GROUPS * half + i, n_chunks, stride=S5_T), :] = x


def _s5_pack(u, n_chunks_pad):
    lp = u.shape[0]
    bw = LANE_GROUPS * S5_CW
    return pl.pallas_call(
        functools.partial(_s5_pack_kernel, n_chunks=lp // S5_T),
        grid=(S5_W // 128,),
        in_specs=[pl.BlockSpec((lp, 128), lambda i: (0, i))],
        out_specs=pl.BlockSpec((n_chunks_pad, bw), lambda i: (0, i)),
        out_shape=jax.ShapeDtypeStruct((n_chunks_pad, S5_G * S5_CW), BF16),
        compiler_params=_cparams("parallel"),
        name="s5_pack",
    )(u)


def _s5_unpack(yc, lp):
    bw = LANE_GROUPS * S5_CW
    return pl.pallas_call(
        functools.partial(_s5_unpack_kernel, n_chunks=lp // S5_T),
        grid=(S5_W // 128,),
        in_specs=[pl.BlockSpec((yc.shape[0], bw), lambda i: (0, i))],
        out_specs=pl.BlockSpec((lp, 128), lambda i: (0, i)),
        out_shape=jax.ShapeDtypeStruct((lp, S5_W), F32),
        compiler_params=_cparams("parallel"),
        name="s5_unpack",
    )(yc)


def _s5_kernel(u_ref, k_ref, b_ref, c_ref, a_ref, y_ref, fre, fim, bre, bim, *, n_chunks):
    pw = 2 * S5_N
    for p in range(S5_PB):
        s = _dot(u_ref[:, p * 2 * S5_CW:(p + 1) * 2 * S5_CW], b_ref[p])
        cols = slice(p * pw, (p + 1) * pw)
        fre[:, cols] = s[:, 0:pw]
        fim[:, cols] = s[:, pw:2 * pw]
        bre[:, cols] = s[:, 2 * pw:3 * pw]
        bim[:, cols] = s[:, 3 * pw:4 * pw]

    far, fai, bar, bai = a_ref[0, 0:1, :], a_ref[0, 1:2, :], a_ref[0, 2:3, :], a_ref[0, 3:4, :]

    def step(i, carry):
        xr, xi, yr, yi = carry
        rf = pl.ds(i, 1)
        rb = pl.ds(n_chunks - 1 - i, 1)
        sr, si = fre[rf, :], fim[rf, :]
        tr, ti = bre[rb, :], bim[rb, :]
        fre[rf, :] = xr
        fim[rf, :] = xi
        bre[rb, :] = yr
        bim[rb, :] = yi
        return (far * xr - fai * xi + sr, far * xi + fai * xr + si,
                bar * yr - bai * yi + tr, bar * yi + bai * yr + ti)

    zero = jnp.zeros((1, S5_PB * pw), F32)
    lax.fori_loop(0, n_chunks, step, (zero, zero, zero, zero), unroll=S5_UNROLL)

    for p in range(S5_PB):
        cols = slice(p * pw, (p + 1) * pw)
        state = jnp.concatenate([fre[:, cols], fim[:, cols], bre[:, cols], bim[:, cols]], axis=1).astype(BF16)
        carry_out = _dot(state, c_ref[p])
        for e in range(2):
            g = 2 * p + e
            gc = slice(g * S5_CW, (g + 1) * S5_CW)
            y = _dot(u_ref[:, gc], k_ref[g]) + carry_out[:, e * S5_CW:(e + 1) * S5_CW]
            y_ref[:, gc] = y.astype(BF16)


def _s5(layer, u_chunks, kmat, bmat, cmat, a_pow):
    n_chunks = u_chunks.shape[0]
    bw = S5_GB * S5_CW
    sw = S5_PB * 2 * S5_N
    return pl.pallas_call(
        functools.partial(_s5_kernel, n_chunks=n_chunks),
        grid=(S5_G // S5_GB,),
        in_specs=[pl.BlockSpec((n_chunks, bw), lambda i: (0, i)),
                  pl.BlockSpec((None, S5_GB, S5_CW, S5_CW), lambda i: (layer, i, 0, 0)),
                  pl.BlockSpec((None, S5_PB, 2 * S5_CW, 8 * S5_N), lambda i: (layer, i, 0, 0)),
                  pl.BlockSpec((None, S5_PB, 8 * S5_N, 2 * S5_CW), lambda i: (layer, i, 0, 0)),
                  pl.BlockSpec((None, 1, 4, sw), lambda i: (layer, i, 0, 0))],
        out_specs=pl.BlockSpec((n_chunks, bw), lambda i: (0, i)),
        out_shape=jax.ShapeDtypeStruct(u_chunks.shape, BF16),
        scratch_shapes=[pltpu.VMEM((n_chunks, sw), F32)] * 4,
        compiler_params=_cparams("parallel"),
        name="s5_scan",
    )(u_chunks, kmat, bmat, cmat, a_pow)


def _row_keep(shape, tile_rows):
    row = pl.program_id(0) * tile_rows + lax.broadcasted_iota(jnp.int32, shape, 0)
    return row >= N_DUMMY


def _even_out_kernel(o_ref, ga_ref, y_ref, gb_ref, h_ref, wglu_ref, bglu_ref, wout_ref, gpost_ref, hout_ref):
    ya = o_ref[...].astype(F32) * _silu(ga_ref[...].astype(F32))
    yb = _gelu_tanh(y_ref[...].astype(F32))
    yb = yb * _sigmoid(_dot(yb.astype(BF16), wglu_ref[...]) + bglu_ref[...])
    yb = yb * _silu(gb_ref[...].astype(F32))
    out = _dot(ya.astype(BF16), wout_ref[0:ATTN_W, :]) + _dot(yb.astype(BF16), wout_ref[ATTN_W:, :])
    hnew = h_ref[...] + _rms(out, gpost_ref[...])
    hout_ref[...] = jnp.where(_row_keep(hnew.shape, TM), hnew, 0.0)


def _even_out(layer, i, o, ga, y, gb, hres, wglu, bglu, wout, gpost):
    lp = hres.shape[0]
    row = lambda w: pl.BlockSpec((TM, w), lambda i: (i, 0))
    _full, _full_l = _slab(i), _slab(layer)
    return pl.pallas_call(
        _even_out_kernel,
        grid=(lp // TM,),
        in_specs=[row(ATTN_W), row(ATTN_W), row(S5_W), row(S5_W), row(D_MODEL),
                  _full((S5_W, S5_W)), _full((1, S5_W)), _full((ATTN_W + S5_W, D_MODEL)), _full_l((1, D_MODEL))],
        out_specs=row(D_MODEL),
        out_shape=jax.ShapeDtypeStruct((lp, D_MODEL), F32),
        compiler_params=_cparams("parallel"),
        name="even_out",
    )(o, ga, y, gb, hres, wglu, bglu, wout, gpost)


def _odd_in_kernel(h_ref, gpre_ref, w_ref, xm_ref, z_ref):
    hn = _rms(h_ref[...], gpre_ref[...]).astype(BF16)
    xm_ref[...] = _dot(hn, w_ref[:, 0:ML_W]).astype(BF16)
    z_ref[...] = _dot(hn, w_ref[:, ML_W:]).astype(BF16)


def _odd_in(layer, i, hres, gpre, w_in):
    lp = hres.shape[0]
    row = lambda w: pl.BlockSpec((TM, w), lambda i: (i, 0))
    return pl.pallas_call(
        _odd_in_kernel,
        grid=(lp // TM,),
        in_specs=[row(D_MODEL), _slab(layer)((1, D_MODEL)), _slab(i)((D_MODEL, 2 * ML_W))],
        out_specs=[row(ML_W), row(ML_W)],
        out_shape=[jax.ShapeDtypeStruct((lp, ML_W), BF16)] * 2,
        compiler_params=_cparams("parallel"),
        name="odd_in",
    )(hres, gpre, w_in)


HALO = 16
CONV_BLOCK = 128
GATE_LANES = 128


def _conv_shift_matrix():
    offs = [j - CONV_W // 2 for j in range(CONV_W) if j != CONV_W // 2]
    r = jnp.arange(CONV_BLOCK)[None, :, None]
    c = jnp.arange(CONV_BLOCK + 2 * HALO)[None, None, :]
    off = jnp.asarray(offs)[:, None, None]
    return (c == r + HALO + off).astype(BF16).reshape(len(offs) * CONV_BLOCK, CONV_BLOCK + 2 * HALO)


def _odd_mid_kernel(xm_ref, prev_ref, next_ref, sh_ref, cw_ref, cb_ref, wqk_ref, wv_ref, wg_ref, bg_ref,
                    q_ref, k_ref, v_ref, xc_ref, g_ref, ext):
    i = pl.program_id(0)
    n = pl.num_programs(0)
    xm = xm_ref[...]
    ext[0:HALO, :] = jnp.where(i > 0, prev_ref[...], jnp.zeros((), BF16))
    ext[HALO:HALO + TM, :] = xm
    ext[HALO + TM:, :] = jnp.where(i < n - 1, next_ref[...], jnp.zeros((), BF16))
    centre = CONV_W // 2
    for b in range(TM // CONV_BLOCK):
        rows = slice(b * CONV_BLOCK, (b + 1) * CONV_BLOCK)
        shifted = _dot(sh_ref[...], ext[b * CONV_BLOCK:(b + 1) * CONV_BLOCK + 2 * HALO, :])
        conv = cb_ref[...] + cw_ref[centre:centre + 1, :] * xm[rows].astype(F32)
        for n_tap, j in enumerate(j for j in range(CONV_W) if j != centre):
            conv = conv + cw_ref[j:j + 1, :] * shifted[n_tap * CONV_BLOCK:(n_tap + 1) * CONV_BLOCK]
        xc_ref[rows, :] = _silu(conv).astype(BF16)
    xcb = xc_ref[...]

    for h in range(ML_H):
        cin = slice(h * ML_DV, (h + 1) * ML_DV)
        qk = _dot(xcb[:, cin], wqk_ref[h])
        q_ref[:, h * ML_DK:(h + 1) * ML_DK] = (qk[:, 0:ML_DK] * (ML_DK ** -0.5)).astype(BF16)
        k_ref[:, h * ML_DK:(h + 1) * ML_DK] = qk[:, ML_DK:].astype(BF16)
        v_ref[:, cin] = _dot(xm[:, cin], wv_ref[h]).astype(BF16)

    pre = _dot(xcb, wg_ref[...]) + bg_ref[...]
    lane = lax.broadcasted_iota(jnp.int32, pre.shape, 1)
    keep = _row_keep(pre.shape, TM)
    li = jnp.where(keep, pre, NEG_BIG)
    lf = jnp.where(keep, jnp.minimum(pre, 0.0) - jnp.log(1.0 + jnp.exp(-jnp.abs(pre))), 0.0)
    t = lax.broadcasted_iota(jnp.int32, (TM, TM), 0)
    s = lax.broadcasted_iota(jnp.int32, (TM, TM), 1)
    same = (t // ML_CHUNK) == (s // ML_CHUNK)
    tri_f = jnp.where(same & (s <= t), 1.0, 0.0).astype(BF16)
    tri_b = jnp.where(same & (s >= t), 1.0, 0.0).astype(BF16)
    lf_hi = lf.astype(BF16)
    lf_lo = (lf - lf_hi.astype(F32)).astype(BF16)
    cum_f = _dot(tri_f, lf_hi) + _dot(tri_f, lf_lo)
    cum_b = _dot(tri_b, lf_hi) + _dot(tri_b, lf_lo)
    g_ref[...] = jnp.where(lane < 2 * ML_H, li, jnp.where(lane < 3 * ML_H, cum_f, cum_b))


def _odd_mid(i, xm, cw, cb, wqk, wv, wg, bg):
    lp = xm.shape[0]
    row = lambda w: pl.BlockSpec((TM, w), lambda i: (i, 0))
    _full = _slab(i)
    per = TM // HALO
    last = lp // HALO - 1
    shift = _conv_shift_matrix()
    return pl.pallas_call(
        _odd_mid_kernel,
        grid=(lp // TM,),
        in_specs=[row(ML_W),
                  pl.BlockSpec((HALO, ML_W), lambda i: (jnp.maximum(i * per - 1, 0), 0)),
                  pl.BlockSpec((HALO, ML_W), lambda i: (jnp.minimum((i + 1) * per, last), 0)),
                  pl.BlockSpec(shift.shape, lambda i: (0, 0)),
                  _full((CONV_W, ML_W)), _full((1, ML_W)),
                  _full((ML_H, ML_DV, 2 * ML_DK)), _full((ML_H, ML_DV, ML_DV)),
                  _full((ML_W, GATE_LANES)), _full((1, GATE_LANES))],
        out_specs=[row(ML_H * ML_DK), row(ML_H * ML_DK), row(ML_W), row(ML_W), row(GATE_LANES)],
        out_shape=[jax.ShapeDtypeStruct((lp, ML_H * ML_DK), BF16), jax.ShapeDtypeStruct((lp, ML_H * ML_DK), BF16),
                   jax.ShapeDtypeStruct((lp, ML_W), BF16), jax.ShapeDtypeStruct((lp, ML_W), BF16),
                   jax.ShapeDtypeStruct((lp, GATE_LANES), F32)],
        scratch_shapes=[pltpu.VMEM((TM + 2 * HALO, ML_W), BF16)],
        compiler_params=_cparams("parallel"),
        name="odd_mid",
    )(xm, xm, xm, shift, cw, cb, wqk, wv, wg, bg)


def _mlstm_chunk(q, k, v, g, gt, h, backward, state, m_prev):
    c = ML_CHUNK
    off = ML_H if backward else 0
    li_row = gt[off + h:off + h + 1, :]
    b_row = gt[2 * ML_H + off + h:2 * ML_H + off + h + 1, :]
    a_row = li_row - b_row
    a_col = g[:, off + h:off + h + 1] - g[:, 2 * ML_H + off + h:2 * ML_H + off + h + 1]
    b_last = jnp.broadcast_to(b_row[:, 0:1] if backward else b_row[:, c - 1:c], (1, c))

    kq = _dot_nt(k, q)
    vt_aug = jnp.concatenate([v.T, jnp.ones((ML_AUG, c), BF16)], axis=0)
    qt = q.T.astype(F32)
    m_new = b_last + jnp.maximum(m_prev, jnp.max(a_row, axis=-1, keepdims=True))
    decay = jnp.exp(b_last + m_prev - m_new)
    w_row = jnp.exp(b_last + a_row - m_new)
    update = _dot((vt_aug.astype(F32) * w_row).astype(BF16), k)
    yield decay * state + update, m_new

    s = lax.broadcasted_iota(jnp.int32, (c, c), 0)
    t = lax.broadcasted_iota(jnp.int32, (c, c), 1)
    allowed = (s >= t) if backward else (s <= t)
    dt = jnp.where(allowed, b_row + a_col, NEG)
    inter_log = b_row + m_prev
    m_t = jnp.maximum(jnp.max(dt, axis=0, keepdims=True), inter_log)
    swt = jnp.exp(dt - m_t) * kq
    inter = jnp.exp(inter_log - m_t)
    num = _dot(jnp.concatenate([vt_aug, state.astype(BF16)], axis=1),
               jnp.concatenate([swt.astype(BF16), (qt * inter).astype(BF16)], axis=0))
    yield None

    den = num[ML_DV:ML_DV + 1, :]
    out_t = num[0:ML_DV, :] / jnp.maximum(jnp.abs(den), jnp.exp(-m_t))
    yield out_t.T


def _mlstm_kernel(qf_ref, kf_ref, vf_ref, gf_ref, qb_ref, kb_ref, vb_ref, gb_ref, hf_ref, hb_ref, c_sc, m_sc):
    @pl.when(pl.program_id(0) == 0)
    def _():
        c_sc[...] = jnp.zeros(c_sc.shape, F32)
        m_sc[...] = jnp.zeros(m_sc.shape, F32)

    chunks = []
    for backward, (q_ref, k_ref, v_ref, g_ref, h_ref) in enumerate(
            ((qf_ref, kf_ref, vf_ref, gf_ref, hf_ref), (qb_ref, kb_ref, vb_ref, gb_ref, hb_ref))):
        order = range(ML_STEP_CHUNKS - 1, -1, -1) if backward else range(ML_STEP_CHUNKS)
        gs = [g_ref[c * ML_CHUNK:(c + 1) * ML_CHUNK, :] for c in range(ML_STEP_CHUNKS)]
        gts = [g.T for g in gs]
        for h in range(ML_H):
            slot = backward * ML_H + h
            state, m = c_sc[slot], m_sc[slot]
            for c in order:
                rows = slice(c * ML_CHUNK, (c + 1) * ML_CHUNK)
                gen = _mlstm_chunk(q_ref[rows, h * ML_DK:(h + 1) * ML_DK], k_ref[rows, h * ML_DK:(h + 1) * ML_DK],
                                   v_ref[rows, h * ML_DV:(h + 1) * ML_DV], gs[c], gts[c], h, bool(backward),
                                   state, m)
                state, m = next(gen)
                chunks.append((gen, h_ref, rows, h))
            c_sc[slot] = state
            m_sc[slot] = m
    for gen, _, _, _ in chunks:
        next(gen)
    for gen, h_ref, rows, h in chunks:
        h_ref[rows, h * ML_DV:(h + 1) * ML_DV] = next(gen).astype(BF16)


def _mlstm(q, k, v, gates):
    lp = q.shape[0]
    rows = ML_STEP_CHUNKS * ML_CHUNK
    nch = lp // rows
    fwd = lambda w: pl.BlockSpec((rows, w), lambda i: (i, 0))
    bwd = lambda w: pl.BlockSpec((rows, w), lambda i: (nch - 1 - i, 0))
    qk_w = ML_H * ML_DK
    return pl.pallas_call(
        _mlstm_kernel,
        grid=(nch,),
        in_specs=[fwd(qk_w), fwd(qk_w), fwd(ML_W), fwd(GATE_LANES),
                  bwd(qk_w), bwd(qk_w), bwd(ML_W), bwd(GATE_LANES)],
        out_specs=[fwd(ML_W), bwd(ML_W)],
        out_shape=[jax.ShapeDtypeStruct((lp, ML_W), BF16)] * 2,
        scratch_shapes=[pltpu.VMEM((2 * ML_H, ML_DV + ML_AUG, ML_DK), F32),
                        pltpu.VMEM((2 * ML_H, 1, ML_CHUNK), F32)],
        compiler_params=_cparams("arbitrary"),
        name="mlstm",
    )(q, k, v, gates, q, k, v, gates)


def _odd_out_kernel(hf_ref, hb_ref, xc_ref, z_ref, h_ref, wo_ref, bo_ref, hn_ref, sk_ref, wout_ref, gpost_ref,
                    hout_ref):
    xcb = xc_ref[...]
    xc = xcb.astype(F32)
    og = _sigmoid(_dot(xcb, wo_ref[...]) + bo_ref[...])
    cell = hf_ref[...].astype(F32) + hb_ref[...].astype(F32)
    parts = []
    for h in range(ML_H):
        sl = slice(h * ML_DV, (h + 1) * ML_DV)
        parts.append(_rms(cell[:, sl], hn_ref[:, sl]))
    cell = jnp.concatenate(parts, axis=1)
    hh = (og * cell + sk_ref[...] * xc) * _silu(z_ref[...].astype(F32))
    out = _dot(hh.astype(BF16), wout_ref[...])
    hnew = h_ref[...] + _rms(out, gpost_ref[...])
    hout_ref[...] = jnp.where(_row_keep(hnew.shape, TM), hnew, 0.0)


def _odd_out(layer, i, hf, hb, xc, z, hres, wo, bo, hn, sk, wout, gpost):
    lp = hres.shape[0]
    row = lambda w: pl.BlockSpec((TM, w), lambda i: (i, 0))
    _full, _full_l = _slab(i), _slab(layer)
    return pl.pallas_call(
        _odd_out_kernel,
        grid=(lp // TM,),
        in_specs=[row(ML_W), row(ML_W), row(ML_W), row(ML_W), row(D_MODEL),
                  _full((ML_W, ML_W)), _full((1, ML_W)), _full((1, ML_W)), _full((1, ML_W)),
                  _full((ML_W, D_MODEL)), _full_l((1, D_MODEL))],
        out_specs=row(D_MODEL),
        out_shape=jax.ShapeDtypeStruct((lp, D_MODEL), F32),
        compiler_params=_cparams("parallel"),
        name="odd_out",
    )(hf, hb, xc, z, hres, wo, bo, hn, sk, wout, gpost)


def _rope_tables(n_real):
    rows = n_real // GRID_W
    freqs = ROPE_THETA ** (-jnp.arange(0, ROPE_AXIS, 2, dtype=F32) / ROPE_AXIS)
    half = freqs.shape[0]
    ang_r = jnp.arange(rows, dtype=F32)[:, None] * freqs[None]
    ang_c = jnp.arange(GRID_W, dtype=F32)[:, None] * freqs[None]
    per_row = lambda t: jnp.broadcast_to(t[:, None, :], (rows, GRID_W, half)).reshape(n_real, half)
    per_col = lambda t: jnp.broadcast_to(t[None, :, :], (rows, GRID_W, half)).reshape(n_real, half)
    cr, sr = per_row(jnp.cos(ang_r)), per_row(jnp.sin(ang_r))
    cc, sc = per_col(jnp.cos(ang_c)), per_col(jnp.sin(ang_c))
    cos = jnp.concatenate([cr, cr, cc, cc], axis=1)
    sin = jnp.concatenate([-sr, sr, -sc, sc], axis=1)
    cos = jnp.concatenate([jnp.ones((FRONT, HEAD_DIM), F32), cos], axis=0)
    sin = jnp.concatenate([jnp.zeros((FRONT, HEAD_DIM), F32), sin], axis=0)
    return cos, sin


def kernel(x, meta_tokens, norm_pre, norm_post, w_in_even, q_norm, k_norm, ssm_a_re, ssm_a_im, ssm_log_dt, ssm_b_re, ssm_b_im, ssm_c_re, ssm_c_im, ssm_d, w_glu, b_glu, w_out_even, w_in_odd, conv_w, conv_b, w_q, w_k, w_v, w_igate, b_igate, w_fgate, b_fgate, w_ogate, b_ogate, head_norm, mlstm_skip, w_out_odd):
    bsz, n_real, _ = x.shape
    assert bsz == 1 and n_real % TK == 0 and (n_real + FRONT) % TM == 0
    lp = n_real + FRONT
    depth = norm_pre.shape[0]
    hres = jnp.concatenate([jnp.zeros((N_DUMMY, D_MODEL), F32), meta_tokens.astype(F32), x[0]], axis=0)
    cos, sin = _rope_tables(n_real)
    n_chunks_pad = -(-(lp // S5_T) // 16) * 16

    bf = lambda a: a.astype(BF16)
    vec = lambda a: a.reshape(a.shape[0], 1, -1)
    norm_pre, norm_post = vec(norm_pre), vec(norm_post)
    w_in_even = bf(w_in_even)
    wvt = jnp.swapaxes(w_in_even[:, :, ATTN_W + KV_W:ATTN_W + 2 * KV_W], 1, 2)
    qg = vec(q_norm) * (HEAD_DIM ** -0.5 * math.log2(math.e))
    kmat, bmat, cmat, a_pow = jax.vmap(_s5_tables)(ssm_a_re, ssm_a_im, ssm_log_dt, ssm_b_re, ssm_b_im,
                                                   ssm_c_re, ssm_c_im, ssm_d)
    w_glu, w_out_even = bf(w_glu), bf(w_out_even)
    w_in_odd, w_ogate, w_out_odd, w_v = bf(w_in_odd), bf(w_ogate), bf(w_out_odd), bf(w_v)
    wqk = bf(jnp.concatenate([w_q, w_k], axis=3))
    gate_pad = GATE_LANES - 4 * ML_H
    wg = jnp.concatenate([w_igate[:, 0], w_igate[:, 1], w_fgate[:, 0], w_fgate[:, 1]], axis=2)
    wg = bf(jnp.pad(wg, ((0, 0), (0, 0), (0, gate_pad))))
    bg = jnp.concatenate([b_igate[:, 0], b_igate[:, 1], b_fgate[:, 0], b_fgate[:, 1]], axis=1)
    bg = vec(jnp.pad(bg, ((0, 0), (0, gate_pad))))

    for layer in range(depth):
        i = layer // 2
        if layer % 2 == 0:
            q, k, vt, ga, u, gb = _even_in(layer, i, hres, norm_pre, w_in_even, wvt, qg, vec(k_norm), cos, sin)
            logit_bound = (math.sqrt(HEAD_DIM) * math.log2(math.e)
                           * jnp.max(jnp.abs(q_norm[i])) * jnp.max(jnp.abs(k_norm[i])))
            o = lax.cond(logit_bound <= LOGIT_SAFE,
                         lambda q, k, vt: _attention(q, k, vt, False),
                         lambda q, k, vt: _attention(q, k, vt, True), q, k, vt)
            y = _s5_unpack(_s5(i, _s5_pack(u, n_chunks_pad), kmat, bmat, cmat, a_pow), lp)
            hres = _even_out(layer, i, o, ga, y, gb, hres, w_glu, vec(b_glu), w_out_even, norm_post)
        else:
            xm, z = _odd_in(layer, i, hres, norm_pre, w_in_odd)
            qm, km, vm, xc, gates = _odd_mid(i, xm, conv_w, vec(conv_b), wqk, w_v, wg, bg)
            hf, hb = _mlstm(qm, km, vm, gates)
            hres = _odd_out(layer, i, hf, hb, xc, z, hres, w_ogate, vec(b_ogate), vec(head_norm), vec(mlstm_skip),
                            w_out_odd, norm_post)
    return hres[FRONT:].reshape(1, n_real, D_MODEL)
```
